```python
import jax, jax.numpy as jnp
from jax import lax
import numpy as np

D_MODEL = 1024
BATCH = 4
SEQ = 4096
DEPTH = 4
DEC_BATCH = 32
DEC_SEQ = 4
PAST_LEN = 8192
PAGE_SIZE = 128

N_MIX_LAYERS = (DEPTH + 1) // 2
N_ATTN_LAYERS = DEPTH // 2
D_CONV = D_MODEL // 2
CONV_WIDTH = 31
HGRN_HEADS = 4
HGRN_DK = (D_MODEL // 2) // HGRN_HEADS
HGRN_DV = (D_MODEL // 2) // HGRN_HEADS
HGRN_MAX_CHUNK = 16
SB_HEADS = 16
SB_HEAD_DIM = D_MODEL // SB_HEADS
SB_QBLOCK = 128
SB_BIAS_INIT = -8.0
D_FF = 4 * D_MODEL
EPS = 1e-6
EVEN_SIZES = (D_CONV, D_CONV, HGRN_HEADS * HGRN_DK, HGRN_HEADS * HGRN_DK, HGRN_HEADS * HGRN_DV, HGRN_HEADS * HGRN_DV)
D_IN_EVEN = sum(EVEN_SIZES)
SPLIT_EVEN = tuple(int(i) for i in np.cumsum(EVEN_SIZES)[:-1])

kernel_name = "conv_hgrn2_stickbreaking_decoder_step"


def rms_norm(x, g):
    xf = x.astype(jnp.float32)
    y = xf * lax.rsqrt(jnp.mean(xf * xf, axis=-1, keepdims=True) + EPS)
    return (y * g.astype(jnp.float32)).astype(x.dtype)


def layer_norm(x, g, b):
    xf = x.astype(jnp.float32)
    mu = jnp.mean(xf, axis=-1, keepdims=True)
    var = jnp.mean(jnp.square(xf - mu), axis=-1, keepdims=True)
    y = (xf - mu) * lax.rsqrt(var + EPS) * g.astype(jnp.float32) + b.astype(jnp.float32)
    return y.astype(x.dtype)


def conformer_conv(val, gate, buf, w, b, ln_g, ln_b):
    u = val * jax.nn.sigmoid(gate)
    u_pad = jnp.concatenate([buf.astype(u.dtype), u], axis=1)
    y = lax.conv_general_dilated(u_pad, w[:, None, :].astype(u.dtype), window_strides=(1,),
                                 padding='VALID', dimension_numbers=('NWC', 'WIO', 'NWC'),
                                 feature_group_count=D_CONV) + b.astype(u.dtype)
    y = jax.nn.silu(layer_norm(y, ln_g, ln_b))
    return y, u_pad[:, -(CONV_WIDTH - 1):]


def hgrn2_lower_bounds(lb_logits):
    p = jax.nn.softmax(lb_logits.astype(jnp.float32), axis=0)
    return jnp.cumsum(p, axis=0) - p[0]


def hgrn2_scan(q, k, v, logf, s0):
    B, L, H, K = q.shape
    C = max(c for c in range(1, HGRN_MAX_CHUNK + 1) if L % c == 0)
    n = L // C

    def to_chunks(t):
        return t.reshape(B, n, C, H, t.shape[-1]).swapaxes(0, 1)

    causal = jnp.tril(jnp.ones((C, C), dtype=bool))

    def step(S, inp):
        qc, kc, vc, gc = inp
        bcum = jnp.cumsum(gc, axis=1)
        q_in = qc * jnp.exp(bcum)
        k_in = kc * jnp.exp(-bcum)
        att = jnp.where(causal, jnp.einsum('bthk,bshk->bhts', q_in, k_in), 0.0)
        o = jnp.einsum('bhts,bshv->bthv', att, vc) + jnp.einsum('bthk,bhkv->bthv', q_in, S)
        b_last = bcum[:, -1]
        S = jnp.exp(b_last)[..., None] * S + jnp.einsum(
            'bshk,bshv->bhkv', kc * jnp.exp(b_last[:, None] - bcum), vc)
        return S, o

    S, o = lax.scan(step, s0, (to_chunks(q), to_chunks(k), to_chunks(v), to_chunks(logf)))
    return o.swapaxes(0, 1).reshape(B, L, H, v.shape[-1]), S


def hgrn2_mixer(q, fz, i, g, s0, lb, norm_g):
    B, L, _ = q.shape
    f32 = jnp.float32
    lb = lb.reshape(HGRN_HEADS, HGRN_DK)
    z = fz.astype(f32).reshape(B, L, HGRN_HEADS, HGRN_DK)
    f = lb + (1.0 - lb) * jax.nn.sigmoid(z)
    k = (1.0 - lb) * jax.nn.sigmoid(-z)
    o, S = hgrn2_scan(q.astype(f32).reshape(B, L, HGRN_HEADS, HGRN_DK), k,
                      i.astype(f32).reshape(B, L, HGRN_HEADS, HGRN_DV), jnp.log(f), s0.astype(f32))
    o = rms_norm(o, norm_g) * jax.nn.silu(g.astype(f32).reshape(B, L, HGRN_HEADS, HGRN_DV))
    return o.reshape(B, L, HGRN_HEADS * HGRN_DV), S


def stick_breaking_block(q, k, v, offset, bias):
    f32 = jnp.float32
    Tq, Tk = q.shape[1], k.shape[1]
    z = jnp.einsum('bqhd,bkhd->bhqk', q.astype(f32), k.astype(f32)) * (SB_HEAD_DIM ** -0.5) \
        + bias.astype(f32)[None, :, None, None]
    visible = jnp.arange(Tk)[None, :] < (offset + jnp.arange(Tq))[:, None]
    log_1m = jnp.where(visible, jax.nn.log_sigmoid(-z), 0.0)
    between = lax.cumsum(log_1m, axis=3, reverse=True) - log_1m
    w = jnp.where(visible, jnp.exp(jax.nn.log_sigmoid(z) + between), 0.0)
    return jnp.einsum('bhqk,bkhd->bqhd', w, v.astype(f32))


def stick_breaking_prompt(q, k, v, bias):
    B, L, H, D = q.shape
    blk = SB_QBLOCK if L % SB_QBLOCK == 0 else L
    n = L // blk
    qb = q.reshape(B, n, blk, H, D).swapaxes(0, 1)
    offs = jnp.arange(n) * blk
    ob = lax.map(lambda a: stick_breaking_block(a[0], k, v, a[1], bias), (qb, offs))
    return ob.swapaxes(0, 1).reshape(B, L, H, D)


def trunk(x, conv_state, hgrn_state, cache_k, cache_v, page_table, lb, p):
    Bsz, L, _ = x.shape
    new_conv, new_hgrn, new_k, new_v = [], [], [], []
    for l in range(DEPTH):
        h = rms_norm(x, p['norm_mix'][l])
        if l % 2 == 0:
            e = l // 2
            a_val, a_gate, b_q, b_f, b_i, b_g = jnp.split(h @ p['w_in_even'][e], SPLIT_EVEN, axis=-1)
            a_out, cbuf = conformer_conv(a_val, a_gate, conv_state[e], p['conv_w'][e], p['conv_b'][e],
                                         p['conv_ln_g'][e], p['conv_ln_b'][e])
            b_out, s_new = hgrn2_mixer(b_q, b_f, b_i, b_g, hgrn_state[e], lb[e], p['hgrn_norm_g'][e])
            mix = jnp.concatenate([a_out, b_out.astype(a_out.dtype)], axis=-1) @ p['w_out_even'][e]
            new_conv.append(cbuf)
            new_hgrn.append(s_new.astype(x.dtype))
        else:
            a = l // 2
            q, k, v = jnp.split(h @ p['w_qkv'][a], 3, axis=-1)
            q = rms_norm(q.reshape(Bsz, L, SB_HEADS, SB_HEAD_DIM), p['q_norm_g'][a])
            k = rms_norm(k.reshape(Bsz, L, SB_HEADS, SB_HEAD_DIM), p['k_norm_g'][a])
            v = v.reshape(Bsz, L, SB_HEADS, SB_HEAD_DIM)
            if cache_k is None:
                o = stick_breaking_prompt(q, k, v, p['sb_bias'][a])
            else:
                past_k = cache_k[a][page_table].reshape(Bsz, -1, SB_HEADS, SB_HEAD_DIM).astype(k.dtype)
                past_v = cache_v[a][page_table].reshape(Bsz, -1, SB_HEADS, SB_HEAD_DIM).astype(v.dtype)
                o = stick_breaking_block(q, jnp.concatenate([past_k, k], axis=1),
                                         jnp.concatenate([past_v, v], axis=1), past_k.shape[1],
                                         p['sb_bias'][a])
            mix = o.reshape(Bsz, L, D_MODEL).astype(x.dtype) @ p['w_o'][a]
            new_k.append(k)
            new_v.append(v)
        x = x + mix
        h = rms_norm(x, p['norm_ffn'][l])
        x = x + jnp.square(jax.nn.relu(h @ p['w_up'][l])) @ p['w_down'][l]
    return x, jnp.stack(new_conv), jnp.stack(new_hgrn), jnp.stack(new_k), jnp.stack(new_v)


def setup_inputs(seed: int = 0) -> dict:
    key = jax.random.key(seed)
    ks = jax.random.split(key, 24)
    f32 = jnp.float32
    n_pages = PAST_LEN // PAGE_SIZE
    n_phys = (DEC_BATCH * n_pages * 5) // 4

    def nrm(k, shape, scale):
        return jax.random.normal(k, shape, f32) * scale

    return {
        'x_prompt': nrm(ks[0], (BATCH, SEQ, D_MODEL), 1.0),
        'x_sample': nrm(ks[1], (DEC_BATCH, DEC_SEQ, D_MODEL), 1.0),
        'state_conv': nrm(ks[2], (N_MIX_LAYERS, DEC_BATCH, CONV_WIDTH - 1, D_CONV), 0.5),
        'state_hgrn': nrm(ks[3], (N_MIX_LAYERS, DEC_BATCH, HGRN_HEADS, HGRN_DK, HGRN_DV), 0.5),
        'cache_k': nrm(ks[4], (N_ATTN_LAYERS, n_phys, PAGE_SIZE, SB_HEADS, SB_HEAD_DIM), 1.0),
        'cache_v': nrm(ks[5], (N_ATTN_LAYERS, n_phys, PAGE_SIZE, SB_HEADS, SB_HEAD_DIM), 1.0),
        'page_table': jax.random.permutation(ks[6], n_phys)[:DEC_BATCH * n_pages]
                      .reshape(DEC_BATCH, n_pages).astype(jnp.int32),
        'norm_mix': 1.0 + nrm(ks[7], (DEPTH, D_MODEL), 0.02),
        'norm_ffn': 1.0 + nrm(ks[8], (DEPTH, D_MODEL), 0.02),
        'w_in_even': nrm(ks[9], (N_MIX_LAYERS, D_MODEL, D_IN_EVEN), D_MODEL ** -0.5),
        'conv_w': nrm(ks[10], (N_MIX_LAYERS, CONV_WIDTH, D_CONV), CONV_WIDTH ** -0.5),
        'conv_b': nrm(ks[11], (N_MIX_LAYERS, D_CONV), 0.02),
        'conv_ln_g': 1.0 + nrm(ks[12], (N_MIX_LAYERS, D_CONV), 0.02),
        'conv_ln_b': nrm(ks[13], (N_MIX_LAYERS, D_CONV), 0.02),
        'hgrn_lb_logits': 1.0 + nrm(ks[14], (N_MIX_LAYERS, HGRN_HEADS * HGRN_DK), 0.3),
        'hgrn_norm_g': 1.0 + nrm(ks[15], (N_MIX_LAYERS, HGRN_DV), 0.02),
        'w_out_even': nrm(ks[16], (N_MIX_LAYERS, D_MODEL, D_MODEL), D_MODEL ** -0.5),
        'w_qkv': nrm(ks[17], (N_ATTN_LAYERS, D_MODEL, 3 * D_MODEL), D_MODEL ** -0.5),
        'q_norm_g': 1.0 + nrm(ks[18], (N_ATTN_LAYERS, SB_HEAD_DIM), 0.02),
        'k_norm_g': 1.0 + nrm(ks[19], (N_ATTN_LAYERS, SB_HEAD_DIM), 0.02),
        'sb_bias': SB_BIAS_INIT + nrm(ks[23], (N_ATTN_LAYERS, SB_HEADS), 0.5),
        'w_o': nrm(ks[20], (N_ATTN_LAYERS, D_MODEL, D_MODEL), D_MODEL ** -0.5),
        'w_up': nrm(ks[21], (DEPTH, D_MODEL, D_FF), D_MODEL ** -0.5),
        'w_down': nrm(ks[22], (DEPTH, D_FF, D_MODEL), 0.5 * D_FF ** -0.5),
    }


def reference(x_prompt, x_sample, state_conv, state_hgrn, cache_k, cache_v, page_table,
              norm_mix, norm_ffn, w_in_even, conv_w, conv_b, conv_ln_g, conv_ln_b,
              hgrn_lb_logits, hgrn_norm_g, w_out_even, w_qkv, q_norm_g, k_norm_g, sb_bias, w_o,
              w_up, w_down):
    p = dict(norm_mix=norm_mix, norm_ffn=norm_ffn, w_in_even=w_in_even, conv_w=conv_w,
             conv_b=conv_b, conv_ln_g=conv_ln_g, conv_ln_b=conv_ln_b, hgrn_norm_g=hgrn_norm_g,
             w_out_even=w_out_even, w_qkv=w_qkv, q_norm_g=q_norm_g, k_norm_g=k_norm_g,
             sb_bias=sb_bias, w_o=w_o, w_up=w_up, w_down=w_down)
    lb = hgrn2_lower_bounds(hgrn_lb_logits)
    conv0 = jnp.zeros((N_MIX_LAYERS, BATCH, CONV_WIDTH - 1, D_CONV), x_prompt.dtype)
    hgrn0 = jnp.zeros((N_MIX_LAYERS, BATCH, HGRN_HEADS, HGRN_DK, HGRN_DV), jnp.float32)
    y_prompt, conv_p, hgrn_p, k_p, v_p = trunk(x_prompt, conv0, hgrn0, None, None, None, lb, p)
    y_sample, conv_s, hgrn_s, k_s, v_s = trunk(x_sample, state_conv, state_hgrn, cache_k, cache_v,
                                               page_table, lb, p)
    return (y_prompt, y_sample, conv_p, conv_s, hgrn_p, hgrn_s, k_p, v_p, k_s, v_s)
```

```python
import functools

import jax
import jax.numpy as jnp
from jax import lax
from jax.experimental import pallas as pl
from jax.experimental.pallas import tpu as pltpu

F32 = jnp.float32
BF16 = jnp.bfloat16

EPS = 1e-6
HGRN_CHUNK = 128
HGRN_BASE = 16
CONV_HALO = 32
ATT_QBLK = 128
ATT_KBLK = 256
VMEM_LIMIT = 56 * 1024 * 1024


def _cparams(sem):
    return pltpu.CompilerParams(dimension_semantics=sem, vmem_limit_bytes=VMEM_LIMIT)


def _rms(x, g):
    ms = jnp.mean(x * x, axis=-1, keepdims=True)
    return x * lax.rsqrt(ms + EPS) * g


def _dot(a, b):
    return jnp.dot(a, b, preferred_element_type=F32)


def _dot_nt(a, b):
    return lax.dot_general(a, b, (((1,), (1,)), ((), ())), preferred_element_type=F32)


def _split2(x):
    hi = x.astype(BF16)
    lo = (x - hi.astype(F32)).astype(BF16)
    return hi, lo


def _split3(x):
    hi = x.astype(BF16)
    r = x - hi.astype(F32)
    mid = r.astype(BF16)
    lo = (r - mid.astype(F32)).astype(BF16)
    return hi, mid, lo


def _norm_matmul_kernel(x_ref, g_ref, w_ref, o_ref):
    h = _rms(x_ref[...], g_ref[...]).astype(BF16)
    o_ref[...] = _dot(h, w_ref[...])


def _norm_matmul(x, g, w, tm):
    m, d = x.shape
    n = w.shape[1]
    return pl.pallas_call(
        _norm_matmul_kernel,
        grid=(m // tm,),
        in_specs=[pl.BlockSpec((tm, d), lambda i: (i, 0)),
                  pl.BlockSpec((1, d), lambda i: (0, 0)),
                  pl.BlockSpec((d, n), lambda i: (0, 0))],
        out_specs=pl.BlockSpec((tm, n), lambda i: (i, 0)),
        out_shape=jax.ShapeDtypeStruct((m, n), F32),
        compiler_params=_cparams(("parallel",)),
        name="norm_matmul",
    )(x, g, w)


def _qkv_kernel(x_ref, g_ref, w_ref, qg_ref, kg_ref, sel_ref, selt_ref,
                q_ref, k_ref, v_ref, kb_ref, vb_ref, *, d, q_scale):
    h = _rms(x_ref[...], g_ref[...]).astype(BF16)
    sel = sel_ref[...]
    selt = selt_ref[...]
    head_dim = d // sel.shape[1]

    def head_norm(t, gain):
        ss = jnp.dot(t * t, sel, preferred_element_type=F32, precision=lax.Precision.HIGHEST)
        rs = lax.rsqrt(ss * (1.0 / head_dim) + EPS)
        scale = jnp.dot(rs, selt, preferred_element_type=F32, precision=lax.Precision.HIGHEST)
        return t * scale * gain

    q = head_norm(_dot(h, w_ref[:, 0:d]), qg_ref[...])
    q_ref[...] = (q * q_scale).astype(BF16)
    k = head_norm(_dot(h, w_ref[:, d:2 * d]), kg_ref[...])
    k_ref[...] = k
    kb_ref[...] = k.astype(BF16)
    v = _dot(h, w_ref[:, 2 * d:3 * d])
    v_ref[...] = v
    vb_ref[...] = v.astype(BF16)


def _qkv_proj(x, g, w, qg, kg, heads, tm):
    m, d = x.shape
    head_dim = d // heads
    lane_head = jnp.arange(d, dtype=jnp.int32) // head_dim
    sel = (lane_head[:, None] == jnp.arange(heads, dtype=jnp.int32)[None, :]).astype(F32)
    row = lambda i: (i, 0)
    fixed = lambda i: (0, 0)
    outs = pl.pallas_call(
        functools.partial(_qkv_kernel, d=d, q_scale=float(head_dim) ** -0.5),
        grid=(m // tm,),
        in_specs=[pl.BlockSpec((tm, d), row),
                  pl.BlockSpec((1, d), fixed),
                  pl.BlockSpec((d, 3 * d), fixed),
                  pl.BlockSpec((1, d), fixed),
                  pl.BlockSpec((1, d), fixed),
                  pl.BlockSpec((d, heads), fixed),
                  pl.BlockSpec((heads, d), fixed)],
        out_specs=[pl.BlockSpec((tm, d), row)] * 5,
        out_shape=[jax.ShapeDtypeStruct((m, d), BF16),
                   jax.ShapeDtypeStruct((m, d), F32),
                   jax.ShapeDtypeStruct((m, d), F32),
                   jax.ShapeDtypeStruct((m, d), BF16),
                   jax.ShapeDtypeStruct((m, d), BF16)],
        compiler_params=_cparams(("parallel",)),
        name="qkv_proj",
    )(x, g, w, jnp.tile(qg, heads)[None, :], jnp.tile(kg, heads)[None, :], sel, sel.T)
    return outs


def _mix_ffn_kernel(x_ref, m_ref, wo_ref, g_ref, wu_ref, wd_ref, o_ref, *, ff_chunk):
    x1 = x_ref[...] + _dot(m_ref[...].astype(BF16), wo_ref[...])
    h = _rms(x1, g_ref[...]).astype(BF16)
    acc = x1
    for c in range(wu_ref.shape[1] // ff_chunk):
        u = _dot(h, wu_ref[:, c * ff_chunk:(c + 1) * ff_chunk])
        u = jnp.square(jnp.maximum(u, 0.0)).astype(BF16)
        acc = acc + _dot(u, wd_ref[c * ff_chunk:(c + 1) * ff_chunk, :])
    o_ref[...] = acc


def _mix_ffn(x, mix_in, wo, g, wu, wd, tm):
    m, d = x.shape
    dff = wu.shape[1]
    row = lambda i: (i, 0)
    fixed = lambda i: (0, 0)
    return pl.pallas_call(
        functools.partial(_mix_ffn_kernel, ff_chunk=min(dff, 1024)),
        grid=(m // tm,),
        in_specs=[pl.BlockSpec((tm, d), row),
                  pl.BlockSpec((tm, d), row),
                  pl.BlockSpec((d, d), fixed),
                  pl.BlockSpec((1, d), fixed),
                  pl.BlockSpec((d, dff), fixed),
                  pl.BlockSpec((dff, d), fixed)],
        out_specs=pl.BlockSpec((tm, d), row),
        out_shape=jax.ShapeDtypeStruct((m, d), F32),
        compiler_params=_cparams(("parallel",)),
        name="mix_ffn",
    )(x, mix_in, wo, g, wu, wd)


def _sigmoid_pair(z):
    e = jnp.exp(-jnp.abs(z))
    inv = 1.0 / (1.0 + e)
    pos = z >= 0
    return jnp.where(pos, 1.0, e) * inv, jnp.where(pos, e, 1.0) * inv


def _even_mixer_kernel(p_ref, cs_ref, hs_ref, lbl_ref, cw_ref, cb_ref, lng_ref, lnb_ref, ng_ref,
                       ab_ref, cso_ref, hso_ref, ubuf, st, pbuf,
                       *, layer, rows, dc, heads, dk, taps):
    C = HGRN_CHUNK
    hk = heads * dk
    t_idx = pl.program_id(1)
    hist = taps - 1

    @pl.when(t_idx == 0)
    def _init():
        ubuf[...] = jnp.zeros_like(ubuf)
        ubuf[pl.ds(CONV_HALO - hist, hist), :] = cs_ref[0]
        for h in range(heads):
            st[h] = hs_ref[0, h].T

    if rows < C:
        pbuf[...] = jnp.zeros_like(pbuf)
        pbuf[pl.ds(0, rows), :] = p_ref[0]
        src = pbuf
    else:
        src = p_ref.at[0]

    val = src[:, 0:dc]
    gate = src[:, dc:2 * dc]
    u = val * _sigmoid_pair(gate)[0]
    ubuf[pl.ds(CONV_HALO, C), :] = u
    cw = cw_ref[...]
    y = jnp.zeros((C, dc), F32) + cb_ref[...]
    for j in range(taps):
        y = y + cw[j:j + 1, :] * ubuf[pl.ds(CONV_HALO - hist + j, C), :]
    mu = jnp.mean(y, axis=-1, keepdims=True)
    yc = y - mu
    var = jnp.mean(yc * yc, axis=-1, keepdims=True)
    y = yc * lax.rsqrt(var + EPS) * lng_ref[...] + lnb_ref[...]
    a_out = y * _sigmoid_pair(y)[0]

    @pl.when(t_idx == pl.num_programs(1) - 1)
    def _conv_state_out():
        cso_ref[0] = ubuf[pl.ds(CONV_HALO + rows - hist, hist), :]

    ubuf[pl.ds(0, CONV_HALO), :] = ubuf[pl.ds(C, CONV_HALO), :]

    lbl = lbl_ref[...]
    lmax = jnp.max(lbl, axis=0, keepdims=True)
    pe = jnp.exp(lbl - lmax)
    psm = pe / jnp.sum(pe, axis=0, keepdims=True)
    lb = jnp.zeros((1, hk), F32)
    for i in range(1, layer + 1):
        lb = lb + psm[i:i + 1, :]

    o0 = 2 * dc
    q = src[:, o0:o0 + hk]
    z = src[:, o0 + hk:o0 + 2 * hk]
    vin = src[:, o0 + 2 * hk:o0 + 3 * hk]
    og = src[:, o0 + 3 * hk:o0 + 4 * hk]
    sig, nsig = _sigmoid_pair(z)
    f = lb + (1.0 - lb) * sig
    k = (1.0 - lb) * nsig
    logf = jnp.log(f)
    row = lax.broadcasted_iota(jnp.int32, (C, hk), 0)
    if rows < C:
        valid = row < rows
        logf = jnp.where(valid, logf, 0.0)
        k = jnp.where(valid, k, 0.0)
    ri = lax.broadcasted_iota(jnp.int32, (C, C), 0)
    ci = lax.broadcasted_iota(jnp.int32, (C, C), 1)
    tril = jnp.where(ri >= ci, 1.0, 0.0).astype(BF16)
    hi, mid, lo = _split3(logf)
    b = _dot(tril, hi) + _dot(tril, mid) + _dot(tril, lo)

    def row_bcast(r, n):
        return jnp.broadcast_to(b[r:r + 1, :], (n, hk))

    b_end = b[C - 1:C, :]
    q_inter = (q * jnp.exp(b)).astype(BF16)
    k_end = (k * jnp.exp(b_end - b)).astype(BF16)
    dec = jnp.exp(b_end)
    levels = []
    m = C // 2
    while m >= HGRN_BASE:
        ref = jnp.concatenate([row_bcast(blk * 2 * m + m - 1, 2 * m) for blk in range(C // (2 * m))], axis=0)
        is_q = (row & (2 * m - 1)) >= m
        e = jnp.where(is_q, b - ref, ref - b)
        levels.append((m, (jnp.where(is_q, q, k) * jnp.exp(e)).astype(BF16)))
        m //= 2
    base = jnp.concatenate(
        [jnp.zeros((HGRN_BASE, hk), F32)]
        + [row_bcast(blk * HGRN_BASE - 1, HGRN_BASE) for blk in range(1, C // HGRN_BASE)], axis=0)
    bl = b - base
    q_diag = (q * jnp.exp(bl)).astype(BF16)
    k_diag = (k * jnp.exp(-bl)).astype(BF16)
    xor = ri ^ ci
    causal = ci <= ri
    vb = vin.astype(BF16)

    outs = []
    for h in range(heads):
        sl = slice(h * dk, (h + 1) * dk)
        att = _dot_nt(q_diag[:, sl], k_diag[:, sl])
        for m, x in reversed(levels):
            att = jnp.where(xor < m, att, _dot_nt(x[:, sl], x[:, sl]))
        att = jnp.where(causal, att, 0.0)
        s_t = st[h]
        o = _dot(att.astype(BF16), vb[:, sl]) + _dot_nt(q_inter[:, sl], s_t.astype(BF16))
        st[h] = s_t * dec[:, sl] + _dot(vin[:, sl].T.astype(BF16), k_end[:, sl])
        o = _rms(o, ng_ref[...])
        g = og[:, sl]
        outs.append(o * (g * _sigmoid_pair(g)[0]))
    b_out = jnp.concatenate(outs, axis=-1)
    ab = jnp.concatenate([a_out, b_out], axis=-1)
    ab_ref[0] = ab[0:rows, :].astype(ab_ref.dtype)

    @pl.when(t_idx == pl.num_programs(1) - 1)
    def _hgrn_state_out():
        for h in range(heads):
            hso_ref[0, h] = st[h].T


def _even_mixer(proj, conv_state, hgrn_state, lb_logits, cw, cb, lng, lnb, ng, layer):
    bsz, seq, width = proj.shape
    taps, dc = cw.shape
    heads, dk, dv = hgrn_state.shape[1:]
    assert dk == dv and width == 2 * dc + 4 * heads * dk
    rows = min(seq, HGRN_CHUNK)
    assert seq % rows == 0
    nt = seq // rows
    kern = functools.partial(_even_mixer_kernel, layer=layer, rows=rows, dc=dc, heads=heads, dk=dk, taps=taps)
    fixed2 = lambda b, t: (0, 0)
    return pl.pallas_call(
        kern,
        grid=(bsz, nt),
        in_specs=[pl.BlockSpec((1, rows, width), lambda b, t: (b, t, 0)),
                  pl.BlockSpec((1, taps - 1, dc), lambda b, t: (b, 0, 0)),
                  pl.BlockSpec((1, heads, dk, dv), lambda b, t: (b, 0, 0, 0)),
                  pl.BlockSpec(lb_logits.shape, fixed2),
                  pl.BlockSpec((taps, dc), fixed2),
                  pl.BlockSpec((1, dc), fixed2),
                  pl.BlockSpec((1, dc), fixed2),
                  pl.BlockSpec((1, dc), fixed2),
                  pl.BlockSpec((1, dv), fixed2)],
        out_specs=[pl.BlockSpec((1, rows, dc + heads * dv), lambda b, t: (b, t, 0)),
                   pl.BlockSpec((1, taps - 1, dc), lambda b, t: (b, 0, 0)),
                   pl.BlockSpec((1, heads, dk, dv), lambda b, t: (b, 0, 0, 0))],
        out_shape=[jax.ShapeDtypeStruct((bsz, seq, dc + heads * dv), BF16 if rows % 16 == 0 else F32),
                   jax.ShapeDtypeStruct((bsz, taps - 1, dc), F32),
                   jax.ShapeDtypeStruct((bsz, heads, dk, dv), F32)],
        scratch_shapes=[pltpu.VMEM((CONV_HALO + HGRN_CHUNK + CONV_HALO, dc), F32),
                        pltpu.VMEM((heads, dv, dk), F32),
                        pltpu.VMEM((HGRN_CHUNK, width), F32)],
        compiler_params=_cparams(("parallel", "arbitrary")),
        name="even_mixer",
    )(proj, conv_state, hgrn_state, lb_logits, cw, cb[None, :], lng[None, :], lnb[None, :], ng[None, :])


def _log1m_sigmoid(z):
    return -(jnp.maximum(z, 0.0) + jnp.log1p(jnp.exp(-jnp.abs(z))))


def _sb_block(z, vis, carry, upper2, vblk):
    l_raw = _log1m_sigmoid(z)
    l = l_raw if vis is None else jnp.where(vis, l_raw, 0.0)
    hi, lo = _split2(l)
    between = _dot(jnp.concatenate([hi, lo], axis=-1), upper2)
    w = jnp.exp(z + l_raw + between + carry)
    if vis is not None:
        w = jnp.where(vis, w, 0.0)
    out = _dot(w.astype(BF16), vblk)
    carry = carry + between[:, 0:1] + l[:, 0:1]
    return out, carry


def _upper2(n):
    j = lax.broadcasted_iota(jnp.int32, (2 * n, n), 0) & (n - 1)
    s = lax.broadcasted_iota(jnp.int32, (2 * n, n), 1)
    return jnp.where(j > s, 1.0, 0.0).astype(BF16)


def _sb_prompt_kernel(bias_ref, q_ref, k_ref, v_ref, o_ref, *, qblk, kblk):
    h = pl.program_id(1)
    qi = pl.program_id(2)
    bias = bias_ref[h]
    q = q_ref[0, 0]
    upper2 = _upper2(kblk)
    q_pos = qi * qblk + lax.broadcasted_iota(jnp.int32, (qblk, kblk), 0)
    k_off = lax.broadcasted_iota(jnp.int32, (qblk, kblk), 1)
    kb_diag = (qi * qblk) // kblk

    def block(kb, carry, masked):
        c, acc = carry
        start = pl.multiple_of(kb * kblk, kblk)
        kblock = k_ref[0, 0, pl.ds(start, kblk), :]
        vblock = v_ref[0, 0, pl.ds(start, kblk), :]
        z = _dot_nt(q, kblock) + bias
        vis = (start + k_off < q_pos) if masked else None
        out, c = _sb_block(z, vis, c, upper2, vblock)
        return c, acc + out

    carry = (jnp.zeros((qblk, 1), F32), jnp.zeros((qblk, v_ref.shape[-1]), F32))
    carry = block(kb_diag, carry, True)
    carry = lax.fori_loop(0, kb_diag, lambda i, cr: block(kb_diag - 1 - i, cr, False), carry)
    o_ref[0, 0] = carry[1].astype(o_ref.dtype)


def _sb_prompt(q, k, v, bias):
    bsz, heads, seq, hd = q.shape
    qblk = min(ATT_QBLK, seq)
    kblk = min(ATT_KBLK, seq)
    assert seq % qblk == 0 and seq % kblk == 0 and kblk % qblk == 0
    return pl.pallas_call(
        functools.partial(_sb_prompt_kernel, qblk=qblk, kblk=kblk),
        grid=(bsz, heads, seq // qblk),
        in_specs=[pl.BlockSpec(memory_space=pltpu.SMEM),
                  pl.BlockSpec((1, 1, qblk, hd), lambda b, h, i: (b, h, i, 0)),
                  pl.BlockSpec((1, 1, seq, hd), lambda b, h, i: (b, h, 0, 0)),
                  pl.BlockSpec((1, 1, seq, hd), lambda b, h, i: (b, h, 0, 0))],
        out_specs=pl.BlockSpec((1, 1, qblk, hd), lambda b, h, i: (b, h, i, 0)),
        out_shape=jax.ShapeDtypeStruct((bsz, heads, seq, hd), BF16),
        compiler_params=_cparams(("parallel", "parallel", "arbitrary")),
        name="sb_prompt",
    )(bias, q, k, v)


def _sb_decode_kernel(pt_ref, bias_ref, q_ref, kn_ref, vn_ref, *rest, heads, pages_per_step, psize, nq):
    kp = rest[:pages_per_step]
    vp = rest[pages_per_step:2 * pages_per_step]
    o_ref = rest[2 * pages_per_step]
    qbd, acc, cbuf, kpad, vpad = rest[2 * pages_per_step + 1:]
    del pt_ref
    step = pl.program_id(1)
    d = q_ref.shape[-1]
    hd = d // heads
    rows = nq * heads
    lane_head = lax.broadcasted_iota(jnp.int32, (rows, d), 1) // hd
    row_head = lax.broadcasted_iota(jnp.int32, (rows, d), 0) & (heads - 1)
    own = lane_head == row_head
    upper2 = _upper2(psize)
    bias = bias_ref[...]

    def attend(kblk, vblk, vis):
        z = _dot_nt(qbd[...], kblk) + bias
        out, c = _sb_block(z, vis, cbuf[...], upper2, vblk)
        acc[...] += out
        cbuf[...] = c

    @pl.when(step == 0)
    def _new_rows():
        qv = q_ref[0]
        qbd[...] = jnp.where(own, jnp.concatenate(
            [jnp.broadcast_to(qv[t:t + 1, :], (heads, d)) for t in range(nq)], axis=0), 0.0).astype(BF16)
        acc[...] = jnp.zeros_like(acc)
        cbuf[...] = jnp.zeros_like(cbuf)
        kpad[...] = jnp.zeros_like(kpad)
        vpad[...] = jnp.zeros_like(vpad)
        kpad[pl.ds(0, nq), :] = kn_ref[0]
        vpad[pl.ds(0, nq), :] = vn_ref[0]
        s_idx = lax.broadcasted_iota(jnp.int32, (rows, psize), 1)
        t_idx = lax.broadcasted_iota(jnp.int32, (rows, psize), 0) // heads
        attend(kpad[...].astype(BF16), vpad[...].astype(BF16), s_idx < t_idx)

    for i in range(pages_per_step):
        attend(kp[i][...].astype(BF16), vp[i][...].astype(BF16), None)

    @pl.when(step == pl.num_programs(1) - 1)
    def _finish():
        a = jnp.where(own, acc[...], 0.0)
        o_ref[0] = jnp.concatenate(
            [jnp.sum(a[t * heads:(t + 1) * heads, :], axis=0, keepdims=True) for t in range(nq)],
            axis=0).astype(o_ref.dtype)


def _sb_decode(q, k_new, v_new, cache_k, cache_v, page_table, bias, layer, heads):
    bsz, nq, d = q.shape
    n_pages = page_table.shape[1]
    psize = cache_k.shape[2]
    pps = 4 if n_pages % 4 == 0 else 1
    steps = n_pages // pps
    rows = nq * heads
    assert heads & (heads - 1) == 0 and psize & (psize - 1) == 0

    def page_spec(i):
        def imap(b, s, pt):
            return (layer, pt[b, n_pages - 1 - (s * pps + i)], 0, 0)
        return pl.BlockSpec((None, None, psize, d), imap)

    tok = lambda b, s, pt: (b, 0, 0)
    bias_rows = jnp.tile(bias, nq)[:, None].astype(F32)
    grid_spec = pltpu.PrefetchScalarGridSpec(
        num_scalar_prefetch=1,
        grid=(bsz, steps),
        in_specs=[pl.BlockSpec((rows, 1), lambda b, s, pt: (0, 0)),
                  pl.BlockSpec((1, nq, d), tok),
                  pl.BlockSpec((1, nq, d), tok),
                  pl.BlockSpec((1, nq, d), tok)]
                 + [page_spec(i) for i in range(pps)] + [page_spec(i) for i in range(pps)],
        out_specs=pl.BlockSpec((1, nq, d), tok),
        scratch_shapes=[pltpu.VMEM((rows, d), BF16),
                        pltpu.VMEM((rows, d), F32),
                        pltpu.VMEM((rows, 1), F32),
                        pltpu.VMEM((psize, d), F32),
                        pltpu.VMEM((psize, d), F32)],
    )
    return pl.pallas_call(
        functools.partial(_sb_decode_kernel, heads=heads, pages_per_step=pps, psize=psize, nq=nq),
        grid_spec=grid_spec,
        out_shape=jax.ShapeDtypeStruct((bsz, nq, d), F32),
        compiler_params=_cparams(("parallel", "arbitrary")),
        name="sb_decode",
    )(page_table, bias_rows, q, k_new, v_new, *([cache_k] * pps), *([cache_v] * pps))


def _trunk(x, conv_state, hgrn_state, cache_k, cache_v, page_table, w, tm):
    bsz, seq, d = x.shape
    depth = w['norm_mix'].shape[0]
    heads = w['sb_bias'].shape[1]
    hd = d // heads
    xf = x.reshape(bsz * seq, d)
    new_conv, new_hgrn, new_k, new_v = [], [], [], []
    for l in range(depth):
        if l % 2 == 0:
            e = l // 2
            proj = _norm_matmul(xf, w['norm_mix'][l][None, :], w['w_in_even'][e], tm)
            ab, cso, hso = _even_mixer(proj.reshape(bsz, seq, -1), conv_state[e], hgrn_state[e],
                                       w['hgrn_lb_logits'], w['conv_w'][e], w['conv_b'][e],
                                       w['conv_ln_g'][e], w['conv_ln_b'][e], w['hgrn_norm_g'][e], e)
            new_conv.append(cso)
            new_hgrn.append(hso)
            mix_in, w_mix = ab.reshape(bsz * seq, d), w['w_out_even'][e]
        else:
            a = l // 2
            q, k, v, kb, vb = _qkv_proj(xf, w['norm_mix'][l][None, :], w['w_qkv'][a],
                                        w['q_norm_g'][a], w['k_norm_g'][a], heads, tm)
            if cache_k is None:
                to_heads = lambda t: t.reshape(bsz, seq, heads, hd).transpose(0, 2, 1, 3)
                o = _sb_prompt(to_heads(q), to_heads(kb), to_heads(vb), w['sb_bias'][a])
                o = o.transpose(0, 2, 1, 3)
            else:
                o = _sb_decode(q.astype(F32).reshape(bsz, seq, d), k.reshape(bsz, seq, d), v.reshape(bsz, seq, d),
                               cache_k, cache_v, page_table, w['sb_bias'][a], a, heads)
            new_k.append(k.reshape(bsz, seq, heads, hd))
            new_v.append(v.reshape(bsz, seq, heads, hd))
            mix_in, w_mix = o.reshape(bsz * seq, d), w['w_o'][a]
        xf = _mix_ffn(xf, mix_in, w_mix, w['norm_ffn'][l][None, :], w['w_up'][l], w['w_down'][l], tm)
    return (xf.reshape(bsz, seq, d), jnp.stack(new_conv), jnp.stack(new_hgrn),
            jnp.stack(new_k), jnp.stack(new_v))


def kernel(x_prompt, x_sample, state_conv, state_hgrn, cache_k, cache_v, page_table, norm_mix, norm_ffn, w_in_even, conv_w, conv_b, conv_ln_g, conv_ln_b, hgrn_lb_logits, hgrn_norm_g, w_out_even, w_qkv, q_norm_g, k_norm_g, sb_bias, w_o, w_up, w_down):
    w = dict(norm_mix=norm_mix, norm_ffn=norm_ffn, conv_w=conv_w, conv_b=conv_b, conv_ln_g=conv_ln_g,
             conv_ln_b=conv_ln_b, hgrn_lb_logits=hgrn_lb_logits, hgrn_norm_g=hgrn_norm_g,
             q_norm_g=q_norm_g, k_norm_g=k_norm_g, sb_bias=sb_bias)
    for name, t in dict(w_in_even=w_in_even, w_out_even=w_out_even, w_qkv=w_qkv, w_o=w_o,
                        w_up=w_up, w_down=w_down).items():
        w[name] = t.astype(BF16)
    bsz, seq, d = x_prompt.shape
    n_mix = state_conv.shape[0]
    conv0 = jnp.zeros((n_mix, bsz) + state_conv.shape[2:], F32)
    hgrn0 = jnp.zeros((n_mix, bsz) + state_hgrn.shape[2:], F32)
    tm_p = 256 if (bsz * seq) % 256 == 0 else bsz * seq
    y_p, conv_p, hgrn_p, k_p, v_p = _trunk(x_prompt, conv0, hgrn0, None, None, None, w, tm_p)
    sb, ss, _ = x_sample.shape
    lay, pages, psize = cache_k.shape[:3]
    ck = cache_k.reshape(lay, pages, psize, d)
    cv = cache_v.reshape(lay, pages, psize, d)
    y_s, conv_s, hgrn_s, k_s, v_s = _trunk(x_sample, state_conv, state_hgrn, ck, cv, page_table, w, sb * ss)
    return (y_p, y_s, conv_p, conv_s, hgrn_p, hgrn_s, k_p, v_p, k_s, v_s)
```

```python
import functools

import jax
import jax.numpy as jnp
from jax import lax
from jax.experimental import pallas as pl
from jax.experimental.pallas import tpu as pltpu

F32 = jnp.float32
BF16 = jnp.bfloat16

EPS = 1e-6
HGRN_CHUNK = 128
HGRN_BASE = 16
CONV_HALO = 32
ATT_BLK = 256
ATT_LANES = 256
DECODE_PAGES_PER_STEP = 4
VMEM_LIMIT = 56 * 1024 * 1024
MXU_TILE = 256
SUBLANES = 8
MASKED_LOGW = -1e30

def _cparams(sem, flags=None):
    return pltpu.CompilerParams(dimension_semantics=sem, vmem_limit_bytes=VMEM_LIMIT, flags=flags)


def _rms(x, g):
    ms = jnp.mean(x * x, axis=-1, keepdims=True)
    return x * lax.rsqrt(ms + EPS) * g


def _dot(a, b):
    return jnp.dot(a, b, preferred_element_type=F32)


def _dot_nt(a, b):
    return lax.dot_general(a, b, (((1,), (1,)), ((), ())), preferred_element_type=F32)


def _split2(x):
    hi = x.astype(BF16)
    lo = (x - hi.astype(F32)).astype(BF16)
    return hi, lo


def _split3(x):
    hi = x.astype(BF16)
    r = x - hi.astype(F32)
    mid = r.astype(BF16)
    lo = (r - mid.astype(F32)).astype(BF16)
    return hi, mid, lo


def _norm_matmul_kernel(x_ref, g_ref, w_ref, o_ref):
    h = _rms(x_ref[...], g_ref[...]).astype(BF16)
    o_ref[...] = _dot(h, w_ref[...])


def _norm_matmul(x, g, w, tm):
    m, d = x.shape
    n = w.shape[1]
    return pl.pallas_call(
        _norm_matmul_kernel,
        grid=(m // tm,),
        in_specs=[pl.BlockSpec((tm, d), lambda i: (i, 0)),
                  pl.BlockSpec((1, d), lambda i: (0, 0)),
                  pl.BlockSpec((d, n), lambda i: (0, 0))],
        out_specs=pl.BlockSpec((tm, n), lambda i: (i, 0)),
        out_shape=jax.ShapeDtypeStruct((m, n), F32),
        compiler_params=_cparams(("parallel",)),
        name="norm_matmul",
    )(x, g, w)


def _qkv_kernel(x_ref, g_ref, w_ref, qg_ref, kg_ref, q_ref, k_ref, v_ref, kb_ref, vb_ref,
                *, d, head_dim, q_scale):
    h = _rms(x_ref[...], g_ref[...]).astype(BF16)
    grp = MXU_TILE
    ri = lax.broadcasted_iota(jnp.int32, (2 * grp, grp), 0) & (grp - 1)
    ci = lax.broadcasted_iota(jnp.int32, (2 * grp, grp), 1)
    same_head2 = jnp.where(ri // head_dim == ci // head_dim, 1.0, 0.0).astype(BF16)

    def head_norm(t, gain):
        t2 = t * t
        parts = []
        for c in range(d // grp):
            hi, lo = _split2(t2[:, c * grp:(c + 1) * grp])
            parts.append(_dot(jnp.concatenate([hi, lo], axis=-1), same_head2))
        ss = jnp.concatenate(parts, axis=-1)
        return t * lax.rsqrt(ss * (1.0 / head_dim) + EPS) * gain

    q = head_norm(_dot(h, w_ref[:, 0:d]), qg_ref[...])
    q_ref[...] = (q * q_scale).astype(BF16)
    k = head_norm(_dot(h, w_ref[:, d:2 * d]), kg_ref[...])
    k_ref[...] = k.reshape(k_ref.shape)
    kb_ref[...] = k.astype(BF16)
    v = _dot(h, w_ref[:, 2 * d:3 * d])
    v_ref[...] = v.reshape(v_ref.shape)
    vb_ref[...] = v.astype(BF16)


def _qkv_proj(x, g, w, qg, kg, heads, tm):
    m, d = x.shape
    head_dim = d // heads
    assert d % MXU_TILE == 0 and MXU_TILE % head_dim == 0
    row = lambda i: (i, 0)
    fixed = lambda i: (0, 0)
    outs = pl.pallas_call(
        functools.partial(_qkv_kernel, d=d, head_dim=head_dim, q_scale=float(head_dim) ** -0.5),
        grid=(m // tm,),
        in_specs=[pl.BlockSpec((tm, d), row),
                  pl.BlockSpec((1, d), fixed),
                  pl.BlockSpec((d, 3 * d), fixed),
                  pl.BlockSpec((1, d), fixed),
                  pl.BlockSpec((1, d), fixed)],
        out_specs=[pl.BlockSpec((tm, d), row),
                   pl.BlockSpec((tm, heads, head_dim), lambda i: (i, 0, 0)),
                   pl.BlockSpec((tm, heads, head_dim), lambda i: (i, 0, 0)),
                   pl.BlockSpec((tm, d), row),
                   pl.BlockSpec((tm, d), row)],
        out_shape=[jax.ShapeDtypeStruct((m, d), BF16),
                   jax.ShapeDtypeStruct((m, heads, head_dim), F32),
                   jax.ShapeDtypeStruct((m, heads, head_dim), F32),
                   jax.ShapeDtypeStruct((m, d), BF16),
                   jax.ShapeDtypeStruct((m, d), BF16)],
        compiler_params=_cparams(("parallel",)),
        name="qkv_proj",
    )(x, g, w, jnp.tile(qg, heads)[None, :], jnp.tile(kg, heads)[None, :])
    return outs


def _mix_ffn_kernel(x_ref, m_ref, wo_ref, g_ref, wu_ref, wd_ref, o_ref, *, ff_chunk):
    x1 = x_ref[...] + _dot(m_ref[...].astype(BF16), wo_ref[...])
    h = _rms(x1, g_ref[...]).astype(BF16)
    acc = x1
    for c in range(wu_ref.shape[1] // ff_chunk):
        u = _dot(h, wu_ref[:, c * ff_chunk:(c + 1) * ff_chunk])
        u = jnp.square(jnp.maximum(u, 0.0)).astype(BF16)
        acc = acc + _dot(u, wd_ref[c * ff_chunk:(c + 1) * ff_chunk, :])
    o_ref[...] = acc


def _mix_ffn(x, mix_in, wo, g, wu, wd, tm):
    m, d = x.shape
    dff = wu.shape[1]
    row = lambda i: (i, 0)
    fixed = lambda i: (0, 0)
    return pl.pallas_call(
        functools.partial(_mix_ffn_kernel, ff_chunk=min(dff, 1024)),
        grid=(m // tm,),
        in_specs=[pl.BlockSpec((tm, d), row),
                  pl.BlockSpec((tm, d), row),
                  pl.BlockSpec((d, d), fixed),
                  pl.BlockSpec((1, d), fixed),
                  pl.BlockSpec((d, dff), fixed),
                  pl.BlockSpec((dff, d), fixed)],
        out_specs=pl.BlockSpec((tm, d), row),
        out_shape=jax.ShapeDtypeStruct((m, d), F32),
        compiler_params=_cparams(("parallel",)),
        name="mix_ffn",
    )(x, mix_in, wo, g, wu, wd)


def _sigmoid_pair(z):
    e = jnp.exp(-jnp.abs(z))
    inv = 1.0 / (1.0 + e)
    pos = z >= 0
    return jnp.where(pos, 1.0, e) * inv, jnp.where(pos, e, 1.0) * inv


def _even_mixer_kernel(p_ref, cs_ref, hs_ref, lbl_ref, cw_ref, cb_ref, lng_ref, lnb_ref, ng_ref,
                       ab_ref, cso_ref, hso_ref, ubuf, st, pbuf, aligned,
                       *, layer, rows, dc, heads, dk, taps):
    C = HGRN_CHUNK
    hk = heads * dk
    t_idx = pl.program_id(1)
    hist = taps - 1

    @pl.when(t_idx == 0)
    def _init():
        ubuf[...] = jnp.zeros_like(ubuf)
        ubuf[pl.ds(CONV_HALO - hist, hist), :] = cs_ref[0]
        for h in range(heads):
            st[h] = hs_ref[0, h].T

    if rows < C:
        pbuf[...] = jnp.zeros_like(pbuf)
        pbuf[pl.ds(0, rows), :] = p_ref[0]
        src = pbuf
    else:
        src = p_ref.at[0]

    val = src[:, 0:dc]
    gate = src[:, dc:2 * dc]
    u = val * _sigmoid_pair(gate)[0]
    ubuf[pl.ds(CONV_HALO, C), :] = u
    cw = cw_ref[...]
    y = jnp.zeros((C, dc), F32) + cb_ref[...]
    first = CONV_HALO - hist
    for phase in range(SUBLANES):
        offs = [first + j for j in range(taps) if (first + j) % SUBLANES == phase]
        if not offs:
            continue
        n = offs[-1] - offs[0] + C
        aligned[pl.ds(0, n), :] = ubuf[pl.ds(offs[0], n), :]
        for o in offs:
            y = y + cw[o - first:o - first + 1, :] * aligned[pl.ds(o - offs[0], C), :]
    mu = jnp.mean(y, axis=-1, keepdims=True)
    yc = y - mu
    var = jnp.mean(yc * yc, axis=-1, keepdims=True)
    y = yc * lax.rsqrt(var + EPS) * lng_ref[...] + lnb_ref[...]
    a_out = y * _sigmoid_pair(y)[0]

    @pl.when(t_idx == pl.num_programs(1) - 1)
    def _conv_state_out():
        cso_ref[0] = ubuf[pl.ds(CONV_HALO + rows - hist, hist), :]

    ubuf[pl.ds(0, CONV_HALO), :] = ubuf[pl.ds(C, CONV_HALO), :]

    lbl = lbl_ref[...]
    lmax = jnp.max(lbl, axis=0, keepdims=True)
    pe = jnp.exp(lbl - lmax)
    psm = pe / jnp.sum(pe, axis=0, keepdims=True)
    lb = jnp.zeros((1, hk), F32)
    for i in range(1, layer + 1):
        lb = lb + psm[i:i + 1, :]

    o0 = 2 * dc
    q = src[:, o0:o0 + hk]
    z = src[:, o0 + hk:o0 + 2 * hk]
    vin = src[:, o0 + 2 * hk:o0 + 3 * hk]
    og = src[:, o0 + 3 * hk:o0 + 4 * hk]
    sig, nsig = _sigmoid_pair(z)
    f = lb + (1.0 - lb) * sig
    k = (1.0 - lb) * nsig
    logf = jnp.log(f)
    row = lax.broadcasted_iota(jnp.int32, (C, hk), 0)
    if rows < C:
        valid = row < rows
        logf = jnp.where(valid, logf, 0.0)
        k = jnp.where(valid, k, 0.0)
    ri = lax.broadcasted_iota(jnp.int32, (C, C), 0)
    ci = lax.broadcasted_iota(jnp.int32, (C, C), 1)
    tril = jnp.where(ri >= ci, 1.0, 0.0).astype(BF16)
    hi, mid, lo = _split3(logf)
    b = _dot(tril, hi) + _dot(tril, mid) + _dot(tril, lo)

    def row_bcast(r, n):
        return jnp.broadcast_to(b[r:r + 1, :], (n, hk))

    b_end = b[C - 1:C, :]
    q_inter = (q * jnp.exp(b)).astype(BF16)
    k_end = (k * jnp.exp(b_end - b)).astype(BF16)
    dec = jnp.exp(b_end)
    levels = []
    m = C // 2
    while m >= HGRN_BASE:
        ref = jnp.concatenate([row_bcast(blk * 2 * m + m - 1, 2 * m) for blk in range(C // (2 * m))], axis=0)
        is_q = (row & (2 * m - 1)) >= m
        e = jnp.where(is_q, b - ref, ref - b)
        levels.append((m, (jnp.where(is_q, q, k) * jnp.exp(e)).astype(BF16)))
        m //= 2
    base = jnp.concatenate(
        [jnp.zeros((HGRN_BASE, hk), F32)]
        + [row_bcast(blk * HGRN_BASE - 1, HGRN_BASE) for blk in range(1, C // HGRN_BASE)], axis=0)
    bl = b - base
    q_diag = (q * jnp.exp(bl)).astype(BF16)
    k_diag = (k * jnp.exp(-bl)).astype(BF16)
    xor = ri ^ ci
    causal = ci <= ri
    vb = vin.astype(BF16)

    outs = []
    for h in range(heads):
        sl = slice(h * dk, (h + 1) * dk)
        att = _dot_nt(q_diag[:, sl], k_diag[:, sl])
        for m, x in reversed(levels):
            att = jnp.where(xor < m, att, _dot_nt(x[:, sl], x[:, sl]))
        att = jnp.where(causal, att, 0.0)
        s_t = st[h]
        o = _dot(att.astype(BF16), vb[:, sl]) + _dot_nt(q_inter[:, sl], s_t.astype(BF16))
        st[h] = s_t * dec[:, sl] + _dot(vin[:, sl].T.astype(BF16), k_end[:, sl])
        o = _rms(o, ng_ref[...])
        g = og[:, sl]
        outs.append(o * (g * _sigmoid_pair(g)[0]))
    b_out = jnp.concatenate(outs, axis=-1)
    ab = jnp.concatenate([a_out, b_out], axis=-1)
    ab_ref[0] = ab[0:rows, :].astype(ab_ref.dtype)

    @pl.when(t_idx == pl.num_programs(1) - 1)
    def _hgrn_state_out():
        for h in range(heads):
            hso_ref[0, h] = st[h].T


def _even_mixer(proj, conv_state, hgrn_state, lb_logits, cw, cb, lng, lnb, ng, layer):
    bsz, seq, width = proj.shape
    taps, dc = cw.shape
    heads, dk, dv = hgrn_state.shape[1:]
    assert dk == dv and width == 2 * dc + 4 * heads * dk
    rows = min(seq, HGRN_CHUNK)
    assert seq % rows == 0
    nt = seq // rows
    kern = functools.partial(_even_mixer_kernel, layer=layer, rows=rows, dc=dc, heads=heads, dk=dk, taps=taps)
    fixed2 = lambda b, t: (0, 0)
    return pl.pallas_call(
        kern,
        grid=(bsz, nt),
        in_specs=[pl.BlockSpec((1, rows, width), lambda b, t: (b, t, 0)),
                  pl.BlockSpec((1, taps - 1, dc), lambda b, t: (b, 0, 0)),
                  pl.BlockSpec((1, heads, dk, dv), lambda b, t: (b, 0, 0, 0)),
                  pl.BlockSpec(lb_logits.shape, fixed2),
                  pl.BlockSpec((taps, dc), fixed2),
                  pl.BlockSpec((1, dc), fixed2),
                  pl.BlockSpec((1, dc), fixed2),
                  pl.BlockSpec((1, dc), fixed2),
                  pl.BlockSpec((1, dv), fixed2)],
        out_specs=[pl.BlockSpec((1, rows, dc + heads * dv), lambda b, t: (b, t, 0)),
                   pl.BlockSpec((1, taps - 1, dc), lambda b, t: (b, 0, 0)),
                   pl.BlockSpec((1, heads, dk, dv), lambda b, t: (b, 0, 0, 0))],
        out_shape=[jax.ShapeDtypeStruct((bsz, seq, dc + heads * dv), BF16 if rows % 16 == 0 else F32),
                   jax.ShapeDtypeStruct((bsz, taps - 1, dc), F32),
                   jax.ShapeDtypeStruct((bsz, heads, dk, dv), F32)],
        scratch_shapes=[pltpu.VMEM((CONV_HALO + HGRN_CHUNK + CONV_HALO, dc), F32),
                        pltpu.VMEM((heads, dv, dk), F32),
                        pltpu.VMEM((HGRN_CHUNK, width), F32),
                        pltpu.VMEM((HGRN_CHUNK + CONV_HALO, dc), F32)],
        compiler_params=_cparams(("parallel", "arbitrary")),
        name="even_mixer",
    )(proj, conv_state, hgrn_state, lb_logits, cw, cb[None, :], lng[None, :], lnb[None, :], ng[None, :])


def _softplus(z):
    return jnp.maximum(z, 0.0) + jnp.log(1.0 + jnp.exp(-jnp.abs(z)))


def _sb_weights(z, vis, carry, suffix2):
    sp = _softplus(z)
    if vis is not None:
        sp = jnp.where(vis, sp, 0.0)
    incl = _dot(jnp.concatenate(_split2(sp), axis=-1), suffix2)
    w = jnp.exp(z - carry - incl)
    if vis is not None:
        w = jnp.where(vis, w, 0.0)
    return w, carry + jnp.sum(sp, axis=-1, keepdims=True)


def _suffix2(n):
    j = lax.broadcasted_iota(jnp.int32, (2 * n, n), 0) & (n - 1)
    s = lax.broadcasted_iota(jnp.int32, (2 * n, n), 1)
    return jnp.where(j >= s, 1.0, 0.0).astype(BF16)


def _sb_prompt_kernel(bias_ref, q_ref, k_ref, v_ref, o_ref, *, blk, hd):
    lanes = q_ref.shape[-1]
    nh = lanes // hd
    hp = pl.program_id(1)
    qi = pl.program_id(2)
    lane_head = lax.broadcasted_iota(jnp.int32, (blk, lanes), 1) // hd
    qf = q_ref[0].astype(F32)
    qs = [jnp.where(lane_head == j, qf, 0.0).astype(BF16) for j in range(nh)]
    biases = [bias_ref[hp * nh + j] for j in range(nh)]
    suffix2 = _suffix2(blk)
    row = lax.broadcasted_iota(jnp.int32, (blk, blk), 0)
    col = lax.broadcasted_iota(jnp.int32, (blk, blk), 1)

    def block(kb, carry, vis):
        cs, accs = carry
        start = pl.multiple_of(kb * blk, blk)
        kblock = k_ref[0, pl.ds(start, blk), :]
        vblock = v_ref[0, pl.ds(start, blk), :]
        zs = [_dot_nt(qs[j], kblock) + biases[j] for j in range(nh)]
        sps = [_softplus(z) for z in zs]
        zcs = [zs[j] - cs[j] for j in range(nh)]
        if vis is not None:
            sps = [jnp.where(vis, sp, 0.0) for sp in sps]
            zcs = [jnp.where(vis, zc, MASKED_LOGW) for zc in zcs]
        incl = [_dot(jnp.concatenate(_split2(sp), axis=-1), suffix2) for sp in sps]
        ws = [jnp.exp(zcs[j] - incl[j]).astype(BF16) for j in range(nh)]
        accs = [accs[j] + _dot(ws[j], vblock) for j in range(nh)]
        cs = [cs[j] + jnp.sum(sps[j], axis=-1, keepdims=True) for j in range(nh)]
        return cs, accs

    carry = ([jnp.zeros((blk, 1), F32) for _ in range(nh)], [jnp.zeros((blk, lanes), F32) for _ in range(nh)])
    carry = block(qi, carry, col < row)
    _, accs = lax.fori_loop(0, qi, lambda i, cr: block(qi - 1 - i, cr, None), carry)
    out = accs[0]
    for j in range(1, nh):
        out = jnp.where(lane_head == j, accs[j], out)
    o_ref[0] = out.astype(o_ref.dtype)


def _sb_prompt(q, k, v, bias, hd):
    bsz, seq, d = q.shape
    lanes = ATT_LANES
    blk = min(ATT_BLK, seq)
    assert seq % blk == 0 and d % lanes == 0 and lanes % hd == 0 and blk & (blk - 1) == 0
    return pl.pallas_call(
        functools.partial(_sb_prompt_kernel, blk=blk, hd=hd),
        grid=(bsz, d // lanes, seq // blk),
        in_specs=[pl.BlockSpec(memory_space=pltpu.SMEM),
                  pl.BlockSpec((1, blk, lanes), lambda b, h, i: (b, i, h)),
                  pl.BlockSpec((1, seq, lanes), lambda b, h, i: (b, 0, h)),
                  pl.BlockSpec((1, seq, lanes), lambda b, h, i: (b, 0, h))],
        out_specs=pl.BlockSpec((1, blk, lanes), lambda b, h, i: (b, i, h)),
        out_shape=jax.ShapeDtypeStruct((bsz, seq, d), BF16),
        compiler_params=_cparams(("parallel", "parallel", "arbitrary")),
        name="sb_prompt",
    )(bias, q, k, v)


def _sb_decode_kernel(pt_ref, bias_ref, q_ref, kn_ref, vn_ref, *rest, heads, pages_per_step, psize, nq):
    kp = rest[:pages_per_step]
    vp = rest[pages_per_step:2 * pages_per_step]
    o_ref = rest[2 * pages_per_step]
    qbd, acc, cbuf, kpad, vpad = rest[2 * pages_per_step + 1:]
    del pt_ref
    step = pl.program_id(1)
    d = q_ref.shape[-1]
    hd = d // heads
    rows = nq * heads
    lane_head = lax.broadcasted_iota(jnp.int32, (rows, d), 1) // hd
    row_head = lax.broadcasted_iota(jnp.int32, (rows, d), 0) & (heads - 1)
    own = lane_head == row_head
    suffix2 = _suffix2(psize)
    bias = bias_ref[...]

    def attend(kblk, vblk, vis):
        z = _dot_nt(qbd[...], kblk) + bias
        w, c = _sb_weights(z, vis, cbuf[...], suffix2)
        acc[...] += _dot(w.astype(BF16), vblk)
        cbuf[...] = c

    @pl.when(step == 0)
    def _new_rows():
        qv = q_ref[0]
        qbd[...] = jnp.where(own, jnp.concatenate(
            [jnp.broadcast_to(qv[t:t + 1, :], (heads, d)) for t in range(nq)], axis=0), 0.0).astype(BF16)
        acc[...] = jnp.zeros_like(acc)
        cbuf[...] = jnp.zeros_like(cbuf)
        kpad[...] = jnp.zeros_like(kpad)
        vpad[...] = jnp.zeros_like(vpad)
        kpad[pl.ds(0, nq), :] = kn_ref[0]
        vpad[pl.ds(0, nq), :] = vn_ref[0]
        s_idx = lax.broadcasted_iota(jnp.int32, (rows, psize), 1)
        t_idx = lax.broadcasted_iota(jnp.int32, (rows, psize), 0) // heads
        attend(kpad[...].astype(BF16), vpad[...].astype(BF16), s_idx < t_idx)

    for i in range(pages_per_step):
        attend(kp[i][...].reshape(psize, d).astype(BF16), vp[i][...].reshape(psize, d).astype(BF16), None)

    @pl.when(step == pl.num_programs(1) - 1)
    def _finish():
        a = jnp.where(own, acc[...], 0.0)
        o_ref[0] = jnp.concatenate(
            [jnp.sum(a[t * heads:(t + 1) * heads, :], axis=0, keepdims=True) for t in range(nq)],
            axis=0).astype(o_ref.dtype)


def _sb_decode(q, k_new, v_new, cache_k, cache_v, page_table, bias, layer, heads):
    bsz, nq, d = q.shape
    n_pages = page_table.shape[1]
    psize = cache_k.shape[2]
    pps = DECODE_PAGES_PER_STEP if n_pages % DECODE_PAGES_PER_STEP == 0 else 1
    steps = n_pages // pps
    rows = nq * heads
    assert heads & (heads - 1) == 0 and psize & (psize - 1) == 0 and cache_k.shape[3:] == (heads, d // heads)

    def page_spec(i):
        def imap(b, s, pt):
            return (layer, pt[b, n_pages - 1 - (s * pps + i)], 0, 0, 0)
        return pl.BlockSpec((None, None, psize, heads, d // heads), imap)

    tok = lambda b, s, pt: (b, 0, 0)
    bias_rows = jnp.tile(bias, nq)[:, None].astype(F32)
    grid_spec = pltpu.PrefetchScalarGridSpec(
        num_scalar_prefetch=1,
        grid=(bsz, steps),
        in_specs=[pl.BlockSpec((rows, 1), lambda b, s, pt: (0, 0)),
                  pl.BlockSpec((1, nq, d), tok),
                  pl.BlockSpec((1, nq, d), tok),
                  pl.BlockSpec((1, nq, d), tok)]
                 + [page_spec(i) for i in range(pps)] + [page_spec(i) for i in range(pps)],
        out_specs=pl.BlockSpec((1, nq, d), tok),
        scratch_shapes=[pltpu.VMEM((rows, d), BF16),
                        pltpu.VMEM((rows, d), F32),
                        pltpu.VMEM((rows, 1), F32),
                        pltpu.VMEM((psize, d), F32),
                        pltpu.VMEM((psize, d), F32)],
    )
    return pl.pallas_call(
        functools.partial(_sb_decode_kernel, heads=heads, pages_per_step=pps, psize=psize, nq=nq),
        grid_spec=grid_spec,
        out_shape=jax.ShapeDtypeStruct((bsz, nq, d), F32),
        compiler_params=_cparams(("parallel", "arbitrary")),
        name="sb_decode",
    )(page_table, bias_rows, q, k_new, v_new, *([cache_k] * pps), *([cache_v] * pps))


def _trunk(x, conv_state, hgrn_state, cache_k, cache_v, page_table, w, tm):
    bsz, seq, d = x.shape
    depth = w['norm_mix'].shape[0]
    heads = w['sb_bias'].shape[1]
    hd = d // heads
    xf = x.reshape(bsz * seq, d)
    new_conv, new_hgrn, new_k, new_v = [], [], [], []
    for l in range(depth):
        if l % 2 == 0:
            e = l // 2
            proj = _norm_matmul(xf, w['norm_mix'][l][None, :], w['w_in_even'][e], tm)
            ab, cso, hso = _even_mixer(proj.reshape(bsz, seq, -1), conv_state[e], hgrn_state[e],
                                       w['hgrn_lb_logits'], w['conv_w'][e], w['conv_b'][e],
                                       w['conv_ln_g'][e], w['conv_ln_b'][e], w['hgrn_norm_g'][e], e)
            new_conv.append(cso)
            new_hgrn.append(hso)
            mix_in, w_mix = ab.reshape(bsz * seq, d), w['w_out_even'][e]
        else:
            a = l // 2
            q, k, v, kb, vb = _qkv_proj(xf, w['norm_mix'][l][None, :], w['w_qkv'][a],
                                        w['q_norm_g'][a], w['k_norm_g'][a], heads, tm)
            if cache_k is None:
                o = _sb_prompt(q.reshape(bsz, seq, d), kb.reshape(bsz, seq, d), vb.reshape(bsz, seq, d),
                               w['sb_bias'][a], hd)
            else:
                as_rows = lambda t: t.astype(F32).reshape(bsz, seq, d)
                o = _sb_decode(as_rows(q), as_rows(kb), as_rows(vb),
                               cache_k, cache_v, page_table, w['sb_bias'][a], a, heads)
            new_k.append(k.reshape(bsz, seq, heads, hd))
            new_v.append(v.reshape(bsz, seq, heads, hd))
            mix_in, w_mix = o.reshape(bsz * seq, d), w['w_o'][a]
        xf = _mix_ffn(xf, mix_in, w_mix, w['norm_ffn'][l][None, :], w['w_up'][l], w['w_down'][l], tm)
    return (xf.reshape(bsz, seq, d), jnp.stack(new_conv), jnp.stack(new_hgrn),
            jnp.stack(new_k), jnp.stack(new_v))


def kernel(x_prompt, x_sample, state_conv, state_hgrn, cache_k, cache_v, page_table, norm_mix, norm_ffn, w_in_even, conv_w, conv_b, conv_ln_g, conv_ln_b, hgrn_lb_logits, hgrn_norm_g, w_out_even, w_qkv, q_norm_g, k_norm_g, sb_bias, w_o, w_up, w_down):
    w = dict(norm_mix=norm_mix, norm_ffn=norm_ffn, conv_w=conv_w, conv_b=conv_b, conv_ln_g=conv_ln_g,
             conv_ln_b=conv_ln_b, hgrn_lb_logits=hgrn_lb_logits, hgrn_norm_g=hgrn_norm_g,
             q_norm_g=q_norm_g, k_norm_g=k_norm_g, sb_bias=sb_bias)
    for name, t in dict(w_in_even=w_in_even, w_out_even=w_out_even, w_qkv=w_qkv, w_o=w_o,
                        w_up=w_up, w_down=w_down).items():
        w[name] = t.astype(BF16)
    bsz, seq, d = x_prompt.shape
    n_mix = state_conv.shape[0]
    conv0 = jnp.zeros((n_mix, bsz) + state_conv.shape[2:], F32)
    hgrn0 = jnp.zeros((n_mix, bsz) + state_hgrn.shape[2:], F32)
    tm_p = 256 if (bsz * seq) % 256 == 0 else bsz * seq
    y_p, conv_p, hgrn_p, k_p, v_p = _trunk(x_prompt, conv0, hgrn0, None, None, None, w, tm_p)
    sb, ss, _ = x_sample.shape
    y_s, conv_s, hgrn_s, k_s, v_s = _trunk(x_sample, state_conv, state_hgrn, cache_k, cache_v, page_table, w,
                                           sb * ss)
    return (y_p, y_s, conv_p, conv_s, hgrn_p, hgrn_s, k_p, v_p, k_s, v_s)
```

```python
import functools

import jax
import jax.numpy as jnp
from jax import lax
from jax.experimental import pallas as pl
from jax.experimental.pallas import tpu as pltpu

F32 = jnp.float32
BF16 = jnp.bfloat16

EPS = 1e-6
HGRN_CHUNK = 128
HGRN_BASE = 16
CONV_HALO = 32
ATT_BLK = 256
ATT_LANES = 256
DECODE_PAGES_PER_STEP = 8
VMEM_LIMIT = 56 * 1024 * 1024
MXU_TILE = 256
SUBLANES = 8
MASKED_LOGW = -1e30

def _cparams(sem, flags=None):
    return pltpu.CompilerParams(dimension_semantics=sem, vmem_limit_bytes=VMEM_LIMIT, flags=flags)


def _rms(x, g):
    ms = jnp.mean(x * x, axis=-1, keepdims=True)
    return x * lax.rsqrt(ms + EPS) * g


def _dot(a, b):
    return jnp.dot(a, b, preferred_element_type=F32)


def _dot_nt(a, b):
    return lax.dot_general(a, b, (((1,), (1,)), ((), ())), preferred_element_type=F32)


def _split2(x):
    hi = x.astype(BF16)
    lo = (x - hi.astype(F32)).astype(BF16)
    return hi, lo


def _split3(x):
    hi = x.astype(BF16)
    r = x - hi.astype(F32)
    mid = r.astype(BF16)
    lo = (r - mid.astype(F32)).astype(BF16)
    return hi, mid, lo


def _norm_matmul_kernel(x_ref, g_ref, w_ref, o_ref):
    h = _rms(x_ref[...], g_ref[...]).astype(BF16)
    o_ref[...] = _dot(h, w_ref[...])


def _norm_matmul(x, g, w, tm):
    m, d = x.shape
    n = w.shape[1]
    return pl.pallas_call(
        _norm_matmul_kernel,
        grid=(m // tm,),
        in_specs=[pl.BlockSpec((tm, d), lambda i: (i, 0)),
                  pl.BlockSpec((1, d), lambda i: (0, 0)),
                  pl.BlockSpec((d, n), lambda i: (0, 0))],
        out_specs=pl.BlockSpec((tm, n), lambda i: (i, 0)),
        out_shape=jax.ShapeDtypeStruct((m, n), F32),
        compiler_params=_cparams(("parallel",)),
        name="norm_matmul",
    )(x, g, w)


def _qkv_kernel(x_ref, g_ref, w_ref, qg_ref, kg_ref, q_ref, k_ref, v_ref, kb_ref, vb_ref,
                *, d, head_dim, q_scale):
    h = _rms(x_ref[...], g_ref[...]).astype(BF16)
    grp = MXU_TILE
    ri = lax.broadcasted_iota(jnp.int32, (2 * grp, grp), 0) & (grp - 1)
    ci = lax.broadcasted_iota(jnp.int32, (2 * grp, grp), 1)
    same_head2 = jnp.where(ri // head_dim == ci // head_dim, 1.0, 0.0).astype(BF16)

    def head_norm(t, gain):
        t2 = t * t
        parts = []
        for c in range(d // grp):
            hi, lo = _split2(t2[:, c * grp:(c + 1) * grp])
            parts.append(_dot(jnp.concatenate([hi, lo], axis=-1), same_head2))
        ss = jnp.concatenate(parts, axis=-1)
        return t * lax.rsqrt(ss * (1.0 / head_dim) + EPS) * gain

    q = head_norm(_dot(h, w_ref[:, 0:d]), qg_ref[...])
    q_ref[...] = (q * q_scale).astype(BF16)
    k = head_norm(_dot(h, w_ref[:, d:2 * d]), kg_ref[...])
    k_ref[...] = k.reshape(k_ref.shape)
    kb_ref[...] = k.astype(BF16)
    v = _dot(h, w_ref[:, 2 * d:3 * d])
    v_ref[...] = v.reshape(v_ref.shape)
    vb_ref[...] = v.astype(BF16)


def _qkv_proj(x, g, w, qg, kg, heads, tm):
    m, d = x.shape
    head_dim = d // heads
    assert d % MXU_TILE == 0 and MXU_TILE % head_dim == 0
    row = lambda i: (i, 0)
    fixed = lambda i: (0, 0)
    outs = pl.pallas_call(
        functools.partial(_qkv_kernel, d=d, head_dim=head_dim, q_scale=float(head_dim) ** -0.5),
        grid=(m // tm,),
        in_specs=[pl.BlockSpec((tm, d), row),
                  pl.BlockSpec((1, d), fixed),
                  pl.BlockSpec((d, 3 * d), fixed),
                  pl.BlockSpec((1, d), fixed),
                  pl.BlockSpec((1, d), fixed)],
        out_specs=[pl.BlockSpec((tm, d), row),
                   pl.BlockSpec((tm, heads, head_dim), lambda i: (i, 0, 0)),
                   pl.BlockSpec((tm, heads, head_dim), lambda i: (i, 0, 0)),
                   pl.BlockSpec((tm, d), row),
                   pl.BlockSpec((tm, d), row)],
        out_shape=[jax.ShapeDtypeStruct((m, d), BF16),
                   jax.ShapeDtypeStruct((m, heads, head_dim), F32),
                   jax.ShapeDtypeStruct((m, heads, head_dim), F32),
                   jax.ShapeDtypeStruct((m, d), BF16),
                   jax.ShapeDtypeStruct((m, d), BF16)],
        compiler_params=_cparams(("parallel",)),
        name="qkv_proj",
    )(x, g, w, jnp.tile(qg, heads)[None, :], jnp.tile(kg, heads)[None, :])
    return outs


def _mix_ffn_kernel(x_ref, m_ref, wo_ref, g_ref, wu_ref, wd_ref, o_ref, *, ff_chunk):
    x1 = x_ref[...] + _dot(m_ref[...].astype(BF16), wo_ref[...])
    h = _rms(x1, g_ref[...]).astype(BF16)
    acc = x1
    for c in range(wu_ref.shape[1] // ff_chunk):
        u = _dot(h, wu_ref[:, c * ff_chunk:(c + 1) * ff_chunk])
        u = jnp.square(jnp.maximum(u, 0.0)).astype(BF16)
        acc = acc + _dot(u, wd_ref[c * ff_chunk:(c + 1) * ff_chunk, :])
    o_ref[...] = acc


def _mix_ffn(x, mix_in, wo, g, wu, wd, tm):
    m, d = x.shape
    dff = wu.shape[1]
    row = lambda i: (i, 0)
    fixed = lambda i: (0, 0)
    return pl.pallas_call(
        functools.partial(_mix_ffn_kernel, ff_chunk=min(dff, 1024)),
        grid=(m // tm,),
        in_specs=[pl.BlockSpec((tm, d), row),
                  pl.BlockSpec((tm, d), row),
                  pl.BlockSpec((d, d), fixed),
                  pl.BlockSpec((1, d), fixed),
                  pl.BlockSpec((d, dff), fixed),
                  pl.BlockSpec((dff, d), fixed)],
        out_specs=pl.BlockSpec((tm, d), row),
        out_shape=jax.ShapeDtypeStruct((m, d), F32),
        compiler_params=_cparams(("parallel",)),
        name="mix_ffn",
    )(x, mix_in, wo, g, wu, wd)


def _sigmoid_pair(z):
    e = jnp.exp(-jnp.abs(z))
    inv = 1.0 / (1.0 + e)
    pos = z >= 0
    return jnp.where(pos, 1.0, e) * inv, jnp.where(pos, e, 1.0) * inv


def _even_mixer_kernel(p_ref, cs_ref, hs_ref, lbl_ref, cw_ref, cb_ref, lng_ref, lnb_ref, ng_ref,
                       ab_ref, cso_ref, hso_ref, ubuf, st, pbuf, aligned,
                       *, layer, rows, dc, heads, dk, taps):
    C = HGRN_CHUNK
    hk = heads * dk
    t_idx = pl.program_id(1)
    hist = taps - 1

    @pl.when(t_idx == 0)
    def _init():
        ubuf[...] = jnp.zeros_like(ubuf)
        ubuf[pl.ds(CONV_HALO - hist, hist), :] = cs_ref[0]
        for h in range(heads):
            st[h] = hs_ref[0, h].T

    if rows < C:
        pbuf[...] = jnp.zeros_like(pbuf)
        pbuf[pl.ds(0, rows), :] = p_ref[0]
        src = pbuf
    else:
        src = p_ref.at[0]

    val = src[:, 0:dc]
    gate = src[:, dc:2 * dc]
    u = val * _sigmoid_pair(gate)[0]
    ubuf[pl.ds(CONV_HALO, C), :] = u
    cw = cw_ref[...]
    y = jnp.zeros((C, dc), F32) + cb_ref[...]
    first = CONV_HALO - hist
    for phase in range(SUBLANES):
        offs = [first + j for j in range(taps) if (first + j) % SUBLANES == phase]
        if not offs:
            continue
        n = offs[-1] - offs[0] + C
        aligned[pl.ds(0, n), :] = ubuf[pl.ds(offs[0], n), :]
        for o in offs:
            y = y + cw[o - first:o - first + 1, :] * aligned[pl.ds(o - offs[0], C), :]
    mu = jnp.mean(y, axis=-1, keepdims=True)
    yc = y - mu
    var = jnp.mean(yc * yc, axis=-1, keepdims=True)
    y = yc * lax.rsqrt(var + EPS) * lng_ref[...] + lnb_ref[...]
    a_out = y * _sigmoid_pair(y)[0]

    @pl.when(t_idx == pl.num_programs(1) - 1)
    def _conv_state_out():
        cso_ref[0] = ubuf[pl.ds(CONV_HALO + rows - hist, hist), :]

    ubuf[pl.ds(0, CONV_HALO), :] = ubuf[pl.ds(C, CONV_HALO), :]

    lbl = lbl_ref[...]
    lmax = jnp.max(lbl, axis=0, keepdims=True)
    pe = jnp.exp(lbl - lmax)
    psm = pe / jnp.sum(pe, axis=0, keepdims=True)
    lb = jnp.zeros((1, hk), F32)
    for i in range(1, layer + 1):
        lb = lb + psm[i:i + 1, :]

    o0 = 2 * dc
    q = src[:, o0:o0 + hk]
    z = src[:, o0 + hk:o0 + 2 * hk]
    vin = src[:, o0 + 2 * hk:o0 + 3 * hk]
    og = src[:, o0 + 3 * hk:o0 + 4 * hk]
    sig, nsig = _sigmoid_pair(z)
    f = lb + (1.0 - lb) * sig
    k = (1.0 - lb) * nsig
    logf = jnp.log(f)
    row = lax.broadcasted_iota(jnp.int32, (C, hk), 0)
    if rows < C:
        valid = row < rows
        logf = jnp.where(valid, logf, 0.0)
        k = jnp.where(valid, k, 0.0)
    ri = lax.broadcasted_iota(jnp.int32, (C, C), 0)
    ci = lax.broadcasted_iota(jnp.int32, (C, C), 1)
    tril = jnp.where(ri >= ci, 1.0, 0.0).astype(BF16)
    hi, mid, lo = _split3(logf)
    b = _dot(tril, hi) + _dot(tril, mid) + _dot(tril, lo)

    def row_bcast(r, n):
        return jnp.broadcast_to(b[r:r + 1, :], (n, hk))

    b_end = b[C - 1:C, :]
    q_inter = (q * jnp.exp(b)).astype(BF16)
    k_end = (k * jnp.exp(b_end - b)).astype(BF16)
    dec = jnp.exp(b_end)
    levels = []
    m = C // 2
    while m >= HGRN_BASE:
        ref = jnp.concatenate([row_bcast(blk * 2 * m + m - 1, 2 * m) for blk in range(C // (2 * m))], axis=0)
        is_q = (row & (2 * m - 1)) >= m
        e = jnp.where(is_q, b - ref, ref - b)
        levels.append((m, (jnp.where(is_q, q, k) * jnp.exp(e)).astype(BF16)))
        m //= 2
    base = jnp.concatenate(
        [jnp.zeros((HGRN_BASE, hk), F32)]
        + [row_bcast(blk * HGRN_BASE - 1, HGRN_BASE) for blk in range(1, C // HGRN_BASE)], axis=0)
    bl = b - base
    q_diag = (q * jnp.exp(bl)).astype(BF16)
    k_diag = (k * jnp.exp(-bl)).astype(BF16)
    xor = ri ^ ci
    causal = ci <= ri
    vb = vin.astype(BF16)

    outs = []
    for h in range(heads):
        sl = slice(h * dk, (h + 1) * dk)
        att = _dot_nt(q_diag[:, sl], k_diag[:, sl])
        for m, x in reversed(levels):
            att = jnp.where(xor < m, att, _dot_nt(x[:, sl], x[:, sl]))
        att = jnp.where(causal, att, 0.0)
        s_t = st[h]
        o = _dot(att.astype(BF16), vb[:, sl]) + _dot_nt(q_inter[:, sl], s_t.astype(BF16))
        st[h] = s_t * dec[:, sl] + _dot(vin[:, sl].T.astype(BF16), k_end[:, sl])
        o = _rms(o, ng_ref[...])
        g = og[:, sl]
        outs.append(o * (g * _sigmoid_pair(g)[0]))
    b_out = jnp.concatenate(outs, axis=-1)
    ab = jnp.concatenate([a_out, b_out], axis=-1)
    ab_ref[0] = ab[0:rows, :].astype(ab_ref.dtype)

    @pl.when(t_idx == pl.num_programs(1) - 1)
    def _hgrn_state_out():
        for h in range(heads):
            hso_ref[0, h] = st[h].T


def _even_mixer(proj, conv_state, hgrn_state, lb_logits, cw, cb, lng, lnb, ng, layer):
    bsz, seq, width = proj.shape
    taps, dc = cw.shape
    heads, dk, dv = hgrn_state.shape[1:]
    assert dk == dv and width == 2 * dc + 4 * heads * dk
    rows = min(seq, HGRN_CHUNK)
    assert seq % rows == 0
    nt = seq // rows
    kern = functools.partial(_even_mixer_kernel, layer=layer, rows=rows, dc=dc, heads=heads, dk=dk, taps=taps)
    fixed2 = lambda b, t: (0, 0)
    return pl.pallas_call(
        kern,
        grid=(bsz, nt),
        in_specs=[pl.BlockSpec((1, rows, width), lambda b, t: (b, t, 0)),
                  pl.BlockSpec((1, taps - 1, dc), lambda b, t: (b, 0, 0)),
                  pl.BlockSpec((1, heads, dk, dv), lambda b, t: (b, 0, 0, 0)),
                  pl.BlockSpec(lb_logits.shape, fixed2),
                  pl.BlockSpec((taps, dc), fixed2),
                  pl.BlockSpec((1, dc), fixed2),
                  pl.BlockSpec((1, dc), fixed2),
                  pl.BlockSpec((1, dc), fixed2),
                  pl.BlockSpec((1, dv), fixed2)],
        out_specs=[pl.BlockSpec((1, rows, dc + heads * dv), lambda b, t: (b, t, 0)),
                   pl.BlockSpec((1, taps - 1, dc), lambda b, t: (b, 0, 0)),
                   pl.BlockSpec((1, heads, dk, dv), lambda b, t: (b, 0, 0, 0))],
        out_shape=[jax.ShapeDtypeStruct((bsz, seq, dc + heads * dv), BF16 if rows % 16 == 0 else F32),
                   jax.ShapeDtypeStruct((bsz, taps - 1, dc), F32),
                   jax.ShapeDtypeStruct((bsz, heads, dk, dv), F32)],
        scratch_shapes=[pltpu.VMEM((CONV_HALO + HGRN_CHUNK + CONV_HALO, dc), F32),
                        pltpu.VMEM((heads, dv, dk), F32),
                        pltpu.VMEM((HGRN_CHUNK, width), F32),
                        pltpu.VMEM((HGRN_CHUNK + CONV_HALO, dc), F32)],
        compiler_params=_cparams(("parallel", "arbitrary")),
        name="even_mixer",
    )(proj, conv_state, hgrn_state, lb_logits, cw, cb[None, :], lng[None, :], lnb[None, :], ng[None, :])


def _softplus(z):
    return jnp.maximum(z, 0.0) + jnp.log(1.0 + jnp.exp(-jnp.abs(z)))


def _sb_weights(z, vis, carry, suffix2):
    sp = _softplus(z)
    if vis is not None:
        sp = jnp.where(vis, sp, 0.0)
    incl = _dot(jnp.concatenate(_split2(sp), axis=-1), suffix2)
    w = jnp.exp(z - carry - incl)
    if vis is not None:
        w = jnp.where(vis, w, 0.0)
    return w, carry + jnp.sum(sp, axis=-1, keepdims=True)


def _suffix2(n):
    j = lax.broadcasted_iota(jnp.int32, (2 * n, n), 0) & (n - 1)
    s = lax.broadcasted_iota(jnp.int32, (2 * n, n), 1)
    return jnp.where(j >= s, 1.0, 0.0).astype(BF16)


def _sb_prompt_kernel(bias_ref, q_ref, k_ref, v_ref, o_ref, *, blk, hd):
    lanes = q_ref.shape[-1]
    nh = lanes // hd
    hp = pl.program_id(1)
    qi = pl.program_id(2)
    lane_head = lax.broadcasted_iota(jnp.int32, (blk, lanes), 1) // hd
    qf = q_ref[0].astype(F32)
    qs = [jnp.where(lane_head == j, qf, 0.0).astype(BF16) for j in range(nh)]
    biases = [bias_ref[hp * nh + j] for j in range(nh)]
    suffix2 = _suffix2(blk)
    row = lax.broadcasted_iota(jnp.int32, (blk, blk), 0)
    col = lax.broadcasted_iota(jnp.int32, (blk, blk), 1)

    def block(kb, carry, vis):
        cs, accs = carry
        start = pl.multiple_of(kb * blk, blk)
        kblock = k_ref[0, pl.ds(start, blk), :]
        vblock = v_ref[0, pl.ds(start, blk), :]
        zs = [_dot_nt(qs[j], kblock) + biases[j] for j in range(nh)]
        sps = [_softplus(z) for z in zs]
        zcs = [zs[j] - cs[j] for j in range(nh)]
        if vis is not None:
            sps = [jnp.where(vis, sp, 0.0) for sp in sps]
            zcs = [jnp.where(vis, zc, MASKED_LOGW) for zc in zcs]
        incl = [_dot(jnp.concatenate(_split2(sp), axis=-1), suffix2) for sp in sps]
        ws = [jnp.exp(zcs[j] - incl[j]).astype(BF16) for j in range(nh)]
        accs = [accs[j] + _dot(ws[j], vblock) for j in range(nh)]
        cs = [cs[j] + jnp.sum(sps[j], axis=-1, keepdims=True) for j in range(nh)]
        return cs, accs

    carry = ([jnp.zeros((blk, 1), F32) for _ in range(nh)], [jnp.zeros((blk, lanes), F32) for _ in range(nh)])
    carry = block(qi, carry, col < row)
    _, accs = lax.fori_loop(0, qi, lambda i, cr: block(qi - 1 - i, cr, None), carry)
    out = accs[0]
    for j in range(1, nh):
        out = jnp.where(lane_head == j, accs[j], out)
    o_ref[0] = out.astype(o_ref.dtype)


def _sb_prompt(q, k, v, bias, hd):
    bsz, seq, d = q.shape
    lanes = ATT_LANES
    blk = min(ATT_BLK, seq)
    assert seq % blk == 0 and d % lanes == 0 and lanes % hd == 0 and blk & (blk - 1) == 0
    return pl.pallas_call(
        functools.partial(_sb_prompt_kernel, blk=blk, hd=hd),
        grid=(bsz, d // lanes, seq // blk),
        in_specs=[pl.BlockSpec(memory_space=pltpu.SMEM),
                  pl.BlockSpec((1, blk, lanes), lambda b, h, i: (b, i, h)),
                  pl.BlockSpec((1, seq, lanes), lambda b, h, i: (b, 0, h)),
                  pl.BlockSpec((1, seq, lanes), lambda b, h, i: (b, 0, h))],
        out_specs=pl.BlockSpec((1, blk, lanes), lambda b, h, i: (b, i, h)),
        out_shape=jax.ShapeDtypeStruct((bsz, seq, d), BF16),
        compiler_params=_cparams(("parallel", "parallel", "arbitrary")),
        name="sb_prompt",
    )(bias, q, k, v)


def _sb_decode_kernel(pt_ref, bias_ref, q_ref, kn_ref, vn_ref, *rest, heads, pages_per_step, psize, nq):
    kp = rest[:pages_per_step]
    vp = rest[pages_per_step:2 * pages_per_step]
    o_ref = rest[2 * pages_per_step]
    qbd, acc, cbuf, kpad, vpad = rest[2 * pages_per_step + 1:]
    del pt_ref
    step = pl.program_id(1)
    d = q_ref.shape[-1]
    hd = d // heads
    rows = nq * heads
    lane_head = lax.broadcasted_iota(jnp.int32, (rows, d), 1) // hd
    row_head = lax.broadcasted_iota(jnp.int32, (rows, d), 0) & (heads - 1)
    own = lane_head == row_head
    suffix2 = _suffix2(psize)
    bias = bias_ref[...]

    def attend(kblk, vblk, vis, feature_major):
        z = (_dot(qbd[...], kblk) if feature_major else _dot_nt(qbd[...], kblk)) + bias
        w, c = _sb_weights(z, vis, cbuf[...], suffix2)
        wb = w.astype(BF16)
        acc[...] += _dot_nt(wb, vblk) if feature_major else _dot(wb, vblk)
        cbuf[...] = c

    @pl.when(step == 0)
    def _new_rows():
        qv = q_ref[0]
        qbd[...] = jnp.where(own, jnp.concatenate(
            [jnp.broadcast_to(qv[t:t + 1, :], (heads, d)) for t in range(nq)], axis=0), 0.0).astype(BF16)
        acc[...] = jnp.zeros_like(acc)
        cbuf[...] = jnp.zeros_like(cbuf)
        kpad[...] = jnp.zeros_like(kpad)
        vpad[...] = jnp.zeros_like(vpad)
        kpad[pl.ds(0, nq), :] = kn_ref[0]
        vpad[pl.ds(0, nq), :] = vn_ref[0]
        s_idx = lax.broadcasted_iota(jnp.int32, (rows, psize), 1)
        t_idx = lax.broadcasted_iota(jnp.int32, (rows, psize), 0) // heads
        attend(kpad[...].astype(BF16), vpad[...].astype(BF16), s_idx < t_idx, False)

    for i in range(pages_per_step):
        attend(kp[i][...].astype(BF16), vp[i][...].astype(BF16), None, True)

    @pl.when(step == pl.num_programs(1) - 1)
    def _finish():
        a = jnp.where(own, acc[...], 0.0)
        o_ref[0] = jnp.concatenate(
            [jnp.sum(a[t * heads:(t + 1) * heads, :], axis=0, keepdims=True) for t in range(nq)],
            axis=0).astype(o_ref.dtype)


def _sb_decode(q, k_new, v_new, cache_k, cache_v, page_table, bias, layer, heads):
    bsz, nq, d = q.shape
    n_pages = page_table.shape[1]
    psize = cache_k.shape[3]
    pps = DECODE_PAGES_PER_STEP if n_pages % DECODE_PAGES_PER_STEP == 0 else 1
    steps = n_pages // pps
    rows = nq * heads
    assert heads & (heads - 1) == 0 and psize & (psize - 1) == 0 and cache_k.shape[2] == d

    def page_spec(i):
        def imap(b, s, pt):
            return (layer, pt[b, n_pages - 1 - (s * pps + i)], 0, 0)
        return pl.BlockSpec((None, None, d, psize), imap)

    tok = lambda b, s, pt: (b, 0, 0)
    bias_rows = jnp.tile(bias, nq)[:, None].astype(F32)
    grid_spec = pltpu.PrefetchScalarGridSpec(
        num_scalar_prefetch=1,
        grid=(bsz, steps),
        in_specs=[pl.BlockSpec((rows, 1), lambda b, s, pt: (0, 0)),
                  pl.BlockSpec((1, nq, d), tok),
                  pl.BlockSpec((1, nq, d), tok),
                  pl.BlockSpec((1, nq, d), tok)]
                 + [page_spec(i) for i in range(pps)] + [page_spec(i) for i in range(pps)],
        out_specs=pl.BlockSpec((1, nq, d), tok),
        scratch_shapes=[pltpu.VMEM((rows, d), BF16),
                        pltpu.VMEM((rows, d), F32),
                        pltpu.VMEM((rows, 1), F32),
                        pltpu.VMEM((psize, d), F32),
                        pltpu.VMEM((psize, d), F32)],
    )
    return pl.pallas_call(
        functools.partial(_sb_decode_kernel, heads=heads, pages_per_step=pps, psize=psize, nq=nq),
        grid_spec=grid_spec,
        out_shape=jax.ShapeDtypeStruct((bsz, nq, d), F32),
        compiler_params=_cparams(("parallel", "arbitrary")),
        name="sb_decode",
    )(page_table, bias_rows, q, k_new, v_new, *([cache_k] * pps), *([cache_v] * pps))


def _trunk(x, conv_state, hgrn_state, cache_k, cache_v, page_table, w, tm):
    bsz, seq, d = x.shape
    depth = w['norm_mix'].shape[0]
    heads = w['sb_bias'].shape[1]
    hd = d // heads
    xf = x.reshape(bsz * seq, d)
    new_conv, new_hgrn, new_k, new_v = [], [], [], []
    for l in range(depth):
        if l % 2 == 0:
            e = l // 2
            proj = _norm_matmul(xf, w['norm_mix'][l][None, :], w['w_in_even'][e], tm)
            ab, cso, hso = _even_mixer(proj.reshape(bsz, seq, -1), conv_state[e], hgrn_state[e],
                                       w['hgrn_lb_logits'], w['conv_w'][e], w['conv_b'][e],
                                       w['conv_ln_g'][e], w['conv_ln_b'][e], w['hgrn_norm_g'][e], e)
            new_conv.append(cso)
            new_hgrn.append(hso)
            mix_in, w_mix = ab.reshape(bsz * seq, d), w['w_out_even'][e]
        else:
            a = l // 2
            q, k, v, kb, vb = _qkv_proj(xf, w['norm_mix'][l][None, :], w['w_qkv'][a],
                                        w['q_norm_g'][a], w['k_norm_g'][a], heads, tm)
            if cache_k is None:
                o = _sb_prompt(q.reshape(bsz, seq, d), kb.reshape(bsz, seq, d), vb.reshape(bsz, seq, d),
                               w['sb_bias'][a], hd)
            else:
                as_rows = lambda t: t.astype(F32).reshape(bsz, seq, d)
                o = _sb_decode(as_rows(q), as_rows(kb), as_rows(vb),
                               cache_k, cache_v, page_table, w['sb_bias'][a], a, heads)
            new_k.append(k.reshape(bsz, seq, heads, hd))
            new_v.append(v.reshape(bsz, seq, heads, hd))
            mix_in, w_mix = o.reshape(bsz * seq, d), w['w_o'][a]
        xf = _mix_ffn(xf, mix_in, w_mix, w['norm_ffn'][l][None, :], w['w_up'][l], w['w_down'][l], tm)
    return (xf.reshape(bsz, seq, d), jnp.stack(new_conv), jnp.stack(new_hgrn),
            jnp.stack(new_k), jnp.stack(new_v))


def kernel(x_prompt, x_sample, state_conv, state_hgrn, cache_k, cache_v, page_table, norm_mix, norm_ffn, w_in_even, conv_w, conv_b, conv_ln_g, conv_ln_b, hgrn_lb_logits, hgrn_norm_g, w_out_even, w_qkv, q_norm_g, k_norm_g, sb_bias, w_o, w_up, w_down):
    w = dict(norm_mix=norm_mix, norm_ffn=norm_ffn, conv_w=conv_w, conv_b=conv_b, conv_ln_g=conv_ln_g,
             conv_ln_b=conv_ln_b, hgrn_lb_logits=hgrn_lb_logits, hgrn_norm_g=hgrn_norm_g,
             q_norm_g=q_norm_g, k_norm_g=k_norm_g, sb_bias=sb_bias)
    for name, t in dict(w_in_even=w_in_even, w_out_even=w_out_even, w_qkv=w_qkv, w_o=w_o,
                        w_up=w_up, w_down=w_down).items():
        w[name] = t.astype(BF16)
    bsz, seq, d = x_prompt.shape
    n_mix = state_conv.shape[0]
    conv0 = jnp.zeros((n_mix, bsz) + state_conv.shape[2:], F32)
    hgrn0 = jnp.zeros((n_mix, bsz) + state_hgrn.shape[2:], F32)
    tm_p = 256 if (bsz * seq) % 256 == 0 else bsz * seq
    y_p, conv_p, hgrn_p, k_p, v_p = _trunk(x_prompt, conv0, hgrn0, None, None, None, w, tm_p)
    sb, ss, _ = x_sample.shape
    lay, pages, psize = cache_k.shape[:3]
    ck = jnp.transpose(cache_k, (0, 1, 3, 4, 2)).reshape(lay, pages, d, psize)
    cv = jnp.transpose(cache_v, (0, 1, 3, 4, 2)).reshape(lay, pages, d, psize)
    y_s, conv_s, hgrn_s, k_s, v_s = _trunk(x_sample, state_conv, state_hgrn, ck, cv, page_table, w, sb * ss)
    return (y_p, y_s, conv_p, conv_s, hgrn_p, hgrn_s, k_p, v_p, k_s, v_s)
```

```python
import functools

import jax
import jax.numpy as jnp
from jax import lax
from jax.experimental import pallas as pl
from jax.experimental.pallas import tpu as pltpu

F32 = jnp.float32
BF16 = jnp.bfloat16

EPS = 1e-6
HGRN_CHUNK = 128
HGRN_BASE = 16
CONV_HALO = 32
ATT_BLK = 256
ATT_LANES = 256
DECODE_PAGES_PER_STEP = 8
VMEM_LIMIT = 56 * 1024 * 1024
MXU_TILE = 256
SUBLANES = 8
MASKED_LOGW = -1e30

def _cparams(sem, flags=None):
    return pltpu.CompilerParams(dimension_semantics=sem, vmem_limit_bytes=VMEM_LIMIT, flags=flags)


def _rms(x, g):
    ms = jnp.mean(x * x, axis=-1, keepdims=True)
    return x * lax.rsqrt(ms + EPS) * g


def _dot(a, b):
    return jnp.dot(a, b, preferred_element_type=F32)


def _dot_nt(a, b):
    return lax.dot_general(a, b, (((1,), (1,)), ((), ())), preferred_element_type=F32)


def _split2(x):
    hi = x.astype(BF16)
    lo = (x - hi.astype(F32)).astype(BF16)
    return hi, lo


def _split3(x):
    hi = x.astype(BF16)
    r = x - hi.astype(F32)
    mid = r.astype(BF16)
    lo = (r - mid.astype(F32)).astype(BF16)
    return hi, mid, lo


def _norm_matmul_kernel(x_ref, g_ref, w_ref, o_ref):
    h = _rms(x_ref[...], g_ref[...]).astype(BF16)
    o_ref[...] = _dot(h, w_ref[...])


def _norm_matmul(x, g, w, tm):
    m, d = x.shape
    n = w.shape[1]
    return pl.pallas_call(
        _norm_matmul_kernel,
        grid=(m // tm,),
        in_specs=[pl.BlockSpec((tm, d), lambda i: (i, 0)),
                  pl.BlockSpec((1, d), lambda i: (0, 0)),
                  pl.BlockSpec((d, n), lambda i: (0, 0))],
        out_specs=pl.BlockSpec((tm, n), lambda i: (i, 0)),
        out_shape=jax.ShapeDtypeStruct((m, n), F32),
        compiler_params=_cparams(("parallel",)),
        name="norm_matmul",
    )(x, g, w)


def _qkv_kernel(x_ref, g_ref, w_ref, qg_ref, kg_ref, q_ref, k_ref, v_ref, kb_ref, vb_ref,
                *, d, head_dim, q_scale):
    h = _rms(x_ref[...], g_ref[...]).astype(BF16)
    grp = MXU_TILE
    ri = lax.broadcasted_iota(jnp.int32, (2 * grp, grp), 0) & (grp - 1)
    ci = lax.broadcasted_iota(jnp.int32, (2 * grp, grp), 1)
    same_head2 = jnp.where(ri // head_dim == ci // head_dim, 1.0, 0.0).astype(BF16)

    def head_norm(t, gain):
        t2 = t * t
        parts = []
        for c in range(d // grp):
            hi, lo = _split2(t2[:, c * grp:(c + 1) * grp])
            parts.append(_dot(jnp.concatenate([hi, lo], axis=-1), same_head2))
        ss = jnp.concatenate(parts, axis=-1)
        return t * lax.rsqrt(ss * (1.0 / head_dim) + EPS) * gain

    q = head_norm(_dot(h, w_ref[:, 0:d]), qg_ref[...])
    q_ref[...] = (q * q_scale).astype(BF16)
    k = head_norm(_dot(h, w_ref[:, d:2 * d]), kg_ref[...])
    k_ref[...] = k.reshape(k_ref.shape)
    kb_ref[...] = k.astype(BF16)
    v = _dot(h, w_ref[:, 2 * d:3 * d])
    v_ref[...] = v.reshape(v_ref.shape)
    vb_ref[...] = v.astype(BF16)


def _qkv_proj(x, g, w, qg, kg, heads, tm):
    m, d = x.shape
    head_dim = d // heads
    assert d % MXU_TILE == 0 and MXU_TILE % head_dim == 0
    row = lambda i: (i, 0)
    fixed = lambda i: (0, 0)
    outs = pl.pallas_call(
        functools.partial(_qkv_kernel, d=d, head_dim=head_dim, q_scale=float(head_dim) ** -0.5),
        grid=(m // tm,),
        in_specs=[pl.BlockSpec((tm, d), row),
                  pl.BlockSpec((1, d), fixed),
                  pl.BlockSpec((d, 3 * d), fixed),
                  pl.BlockSpec((1, d), fixed),
                  pl.BlockSpec((1, d), fixed)],
        out_specs=[pl.BlockSpec((tm, d), row),
                   pl.BlockSpec((tm, heads, head_dim), lambda i: (i, 0, 0)),
                   pl.BlockSpec((tm, heads, head_dim), lambda i: (i, 0, 0)),
                   pl.BlockSpec((tm, d), row),
                   pl.BlockSpec((tm, d), row)],
        out_shape=[jax.ShapeDtypeStruct((m, d), BF16),
                   jax.ShapeDtypeStruct((m, heads, head_dim), F32),
                   jax.ShapeDtypeStruct((m, heads, head_dim), F32),
                   jax.ShapeDtypeStruct((m, d), BF16),
                   jax.ShapeDtypeStruct((m, d), BF16)],
        compiler_params=_cparams(("parallel",)),
        name="qkv_proj",
    )(x, g, w, jnp.tile(qg, heads)[None, :], jnp.tile(kg, heads)[None, :])
    return outs


def _mix_ffn_kernel(x_ref, m_ref, wo_ref, g_ref, wu_ref, wd_ref, o_ref, *, ff_chunk):
    x1 = x_ref[...] + _dot(m_ref[...].astype(BF16), wo_ref[...])
    h = _rms(x1, g_ref[...]).astype(BF16)
    acc = x1
    for c in range(wu_ref.shape[1] // ff_chunk):
        u = _dot(h, wu_ref[:, c * ff_chunk:(c + 1) * ff_chunk])
        u = jnp.square(jnp.maximum(u, 0.0)).astype(BF16)
        acc = acc + _dot(u, wd_ref[c * ff_chunk:(c + 1) * ff_chunk, :])
    o_ref[...] = acc


def _mix_ffn(x, mix_in, wo, g, wu, wd, tm):
    m, d = x.shape
    dff = wu.shape[1]
    row = lambda i: (i, 0)
    fixed = lambda i: (0, 0)
    return pl.pallas_call(
        functools.partial(_mix_ffn_kernel, ff_chunk=min(dff, 1024)),
        grid=(m // tm,),
        in_specs=[pl.BlockSpec((tm, d), row),
                  pl.BlockSpec((tm, d), row),
                  pl.BlockSpec((d, d), fixed),
                  pl.BlockSpec((1, d), fixed),
                  pl.BlockSpec((d, dff), fixed),
                  pl.BlockSpec((dff, d), fixed)],
        out_specs=pl.BlockSpec((tm, d), row),
        out_shape=jax.ShapeDtypeStruct((m, d), F32),
        compiler_params=_cparams(("parallel",)),
        name="mix_ffn",
    )(x, mix_in, wo, g, wu, wd)


def _sigmoid_pair(z):
    e = jnp.exp(-jnp.abs(z))
    inv = 1.0 / (1.0 + e)
    pos = z >= 0
    return jnp.where(pos, 1.0, e) * inv, jnp.where(pos, e, 1.0) * inv


def _even_mixer_kernel(p_ref, cs_ref, hs_ref, lbl_ref, cw_ref, cb_ref, lng_ref, lnb_ref, ng_ref,
                       ab_ref, cso_ref, hso_ref, ubuf, st, pbuf, aligned,
                       *, layer, rows, dc, heads, dk, taps):
    C = HGRN_CHUNK
    hk = heads * dk
    t_idx = pl.program_id(1)
    hist = taps - 1

    @pl.when(t_idx == 0)
    def _init():
        ubuf[...] = jnp.zeros_like(ubuf)
        ubuf[pl.ds(CONV_HALO - hist, hist), :] = cs_ref[0]
        for h in range(heads):
            st[h] = hs_ref[0, h].T

    if rows < C:
        pbuf[...] = jnp.zeros_like(pbuf)
        pbuf[pl.ds(0, rows), :] = p_ref[0]
        src = pbuf
    else:
        src = p_ref.at[0]

    val = src[:, 0:dc]
    gate = src[:, dc:2 * dc]
    u = val * _sigmoid_pair(gate)[0]
    ubuf[pl.ds(CONV_HALO, C), :] = u
    cw = cw_ref[...]
    y = jnp.zeros((C, dc), F32) + cb_ref[...]
    first = CONV_HALO - hist
    for phase in range(SUBLANES):
        offs = [first + j for j in range(taps) if (first + j) % SUBLANES == phase]
        if not offs:
            continue
        n = offs[-1] - offs[0] + C
        aligned[pl.ds(0, n), :] = ubuf[pl.ds(offs[0], n), :]
        for o in offs:
            y = y + cw[o - first:o - first + 1, :] * aligned[pl.ds(o - offs[0], C), :]
    mu = jnp.mean(y, axis=-1, keepdims=True)
    yc = y - mu
    var = jnp.mean(yc * yc, axis=-1, keepdims=True)
    y = yc * lax.rsqrt(var + EPS) * lng_ref[...] + lnb_ref[...]
    a_out = y * _sigmoid_pair(y)[0]

    @pl.when(t_idx == pl.num_programs(1) - 1)
    def _conv_state_out():
        cso_ref[0] = ubuf[pl.ds(CONV_HALO + rows - hist, hist), :]

    ubuf[pl.ds(0, CONV_HALO), :] = ubuf[pl.ds(C, CONV_HALO), :]

    lbl = lbl_ref[...]
    lmax = jnp.max(lbl, axis=0, keepdims=True)
    pe = jnp.exp(lbl - lmax)
    psm = pe / jnp.sum(pe, axis=0, keepdims=True)
    lb = jnp.zeros((1, hk), F32)
    for i in range(1, layer + 1):
        lb = lb + psm[i:i + 1, :]

    o0 = 2 * dc
    q = src[:, o0:o0 + hk]
    z = src[:, o0 + hk:o0 + 2 * hk]
    vin = src[:, o0 + 2 * hk:o0 + 3 * hk]
    og = src[:, o0 + 3 * hk:o0 + 4 * hk]
    sig, nsig = _sigmoid_pair(z)
    f = lb + (1.0 - lb) * sig
    k = (1.0 - lb) * nsig
    logf = jnp.log(f)
    row = lax.broadcasted_iota(jnp.int32, (C, hk), 0)
    if rows < C:
        valid = row < rows
        logf = jnp.where(valid, logf, 0.0)
        k = jnp.where(valid, k, 0.0)
    ri = lax.broadcasted_iota(jnp.int32, (C, C), 0)
    ci = lax.broadcasted_iota(jnp.int32, (C, C), 1)
    tril = jnp.where(ri >= ci, 1.0, 0.0).astype(BF16)
    hi, mid, lo = _split3(logf)
    b = _dot(tril, hi) + _dot(tril, mid) + _dot(tril, lo)

    def row_bcast(r, n):
        return jnp.broadcast_to(b[r:r + 1, :], (n, hk))

    b_end = b[C - 1:C, :]
    q_inter = (q * jnp.exp(b)).astype(BF16)
    k_end = (k * jnp.exp(b_end - b)).astype(BF16)
    dec = jnp.exp(b_end)
    levels = []
    m = C // 2
    while m >= HGRN_BASE:
        ref = jnp.concatenate([row_bcast(blk * 2 * m + m - 1, 2 * m) for blk in range(C // (2 * m))], axis=0)
        is_q = (row & (2 * m - 1)) >= m
        e = jnp.where(is_q, b - ref, ref - b)
        levels.append((m, (jnp.where(is_q, q, k) * jnp.exp(e)).astype(BF16)))
        m //= 2
    base = jnp.concatenate(
        [jnp.zeros((HGRN_BASE, hk), F32)]
        + [row_bcast(blk * HGRN_BASE - 1, HGRN_BASE) for blk in range(1, C // HGRN_BASE)], axis=0)
    bl = b - base
    q_diag = (q * jnp.exp(bl)).astype(BF16)
    k_diag = (k * jnp.exp(-bl)).astype(BF16)
    xor = ri ^ ci
    causal = ci <= ri
    vb = vin.astype(BF16)

    outs = []
    for h in range(heads):
        sl = slice(h * dk, (h + 1) * dk)
        att = _dot_nt(q_diag[:, sl], k_diag[:, sl])
        for m, x in reversed(levels):
            att = jnp.where(xor < m, att, _dot_nt(x[:, sl], x[:, sl]))
        att = jnp.where(causal, att, 0.0)
        s_t = st[h]
        o = _dot(att.astype(BF16), vb[:, sl]) + _dot_nt(q_inter[:, sl], s_t.astype(BF16))
        st[h] = s_t * dec[:, sl] + _dot(vin[:, sl].T.astype(BF16), k_end[:, sl])
        o = _rms(o, ng_ref[...])
        g = og[:, sl]
        outs.append(o * (g * _sigmoid_pair(g)[0]))
    b_out = jnp.concatenate(outs, axis=-1)
    ab = jnp.concatenate([a_out, b_out], axis=-1)
    ab_ref[0] = ab[0:rows, :].astype(ab_ref.dtype)

    @pl.when(t_idx == pl.num_programs(1) - 1)
    def _hgrn_state_out():
        for h in range(heads):
            hso_ref[0, h] = st[h].T


def _even_mixer(proj, conv_state, hgrn_state, lb_logits, cw, cb, lng, lnb, ng, layer):
    bsz, seq, width = proj.shape
    taps, dc = cw.shape
    heads, dk, dv = hgrn_state.shape[1:]
    assert dk == dv and width == 2 * dc + 4 * heads * dk
    rows = min(seq, HGRN_CHUNK)
    assert seq % rows == 0
    nt = seq // rows
    kern = functools.partial(_even_mixer_kernel, layer=layer, rows=rows, dc=dc, heads=heads, dk=dk, taps=taps)
    fixed2 = lambda b, t: (0, 0)
    return pl.pallas_call(
        kern,
        grid=(bsz, nt),
        in_specs=[pl.BlockSpec((1, rows, width), lambda b, t: (b, t, 0)),
                  pl.BlockSpec((1, taps - 1, dc), lambda b, t: (b, 0, 0)),
                  pl.BlockSpec((1, heads, dk, dv), lambda b, t: (b, 0, 0, 0)),
                  pl.BlockSpec(lb_logits.shape, fixed2),
                  pl.BlockSpec((taps, dc), fixed2),
                  pl.BlockSpec((1, dc), fixed2),
                  pl.BlockSpec((1, dc), fixed2),
                  pl.BlockSpec((1, dc), fixed2),
                  pl.BlockSpec((1, dv), fixed2)],
        out_specs=[pl.BlockSpec((1, rows, dc + heads * dv), lambda b, t: (b, t, 0)),
                   pl.BlockSpec((1, taps - 1, dc), lambda b, t: (b, 0, 0)),
                   pl.BlockSpec((1, heads, dk, dv), lambda b, t: (b, 0, 0, 0))],
        out_shape=[jax.ShapeDtypeStruct((bsz, seq, dc + heads * dv), BF16 if rows % 16 == 0 else F32),
                   jax.ShapeDtypeStruct((bsz, taps - 1, dc), F32),
                   jax.ShapeDtypeStruct((bsz, heads, dk, dv), F32)],
        scratch_shapes=[pltpu.VMEM((CONV_HALO + HGRN_CHUNK + CONV_HALO, dc), F32),
                        pltpu.VMEM((heads, dv, dk), F32),
                        pltpu.VMEM((HGRN_CHUNK, width), F32),
                        pltpu.VMEM((HGRN_CHUNK + CONV_HALO, dc), F32)],
        compiler_params=_cparams(("parallel", "arbitrary")),
        name="even_mixer",
    )(proj, conv_state, hgrn_state, lb_logits, cw, cb[None, :], lng[None, :], lnb[None, :], ng[None, :])


def _softplus(z):
    return jnp.maximum(z, 0.0) + jnp.log(1.0 + jnp.exp(-jnp.abs(z)))


def _sb_weights(z, vis, carry, suffix2):
    sp = _softplus(z)
    if vis is not None:
        sp = jnp.where(vis, sp, 0.0)
    incl = _dot(jnp.concatenate(_split2(sp), axis=-1), suffix2)
    w = jnp.exp(z - carry - incl)
    if vis is not None:
        w = jnp.where(vis, w, 0.0)
    return w, carry + jnp.sum(sp, axis=-1, keepdims=True)


def _suffix2(n):
    j = lax.broadcasted_iota(jnp.int32, (2 * n, n), 0) & (n - 1)
    s = lax.broadcasted_iota(jnp.int32, (2 * n, n), 1)
    return jnp.where(j >= s, 1.0, 0.0).astype(BF16)


def _sb_prompt_kernel(bias_ref, q_ref, k_ref, v_ref, o_ref, *, blk, hd):
    lanes = q_ref.shape[-1]
    nh = lanes // hd
    hp = pl.program_id(1)
    qi = pl.program_id(2)
    lane_head = lax.broadcasted_iota(jnp.int32, (blk, lanes), 1) // hd
    qf = q_ref[0].astype(F32)
    qs = [jnp.where(lane_head == j, qf, 0.0).astype(BF16) for j in range(nh)]
    biases = [bias_ref[hp * nh + j] for j in range(nh)]
    suffix2 = _suffix2(blk)
    row = lax.broadcasted_iota(jnp.int32, (blk, blk), 0)
    col = lax.broadcasted_iota(jnp.int32, (blk, blk), 1)

    def blocks(kbs, carry, vis):
        cs, accs = carry
        starts = [pl.multiple_of(kb * blk, blk) for kb in kbs]
        zs = [[_dot_nt(qs[j], k_ref[0, pl.ds(st, blk), :]) + biases[j] for j in range(nh)] for st in starts]
        sps = [[_softplus(z) for z in zk] for zk in zs]
        if vis is not None:
            sps[0] = [jnp.where(vis, sp, 0.0) for sp in sps[0]]
        sums = [[jnp.sum(sp, axis=-1, keepdims=True) for sp in sk] for sk in sps]
        zcs = []
        for k in range(len(kbs)):
            zcs.append([zs[k][j] - cs[j] for j in range(nh)])
            cs = [cs[j] + sums[k][j] for j in range(nh)]
        if vis is not None:
            zcs[0] = [jnp.where(vis, zc, MASKED_LOGW) for zc in zcs[0]]
        incl = [[_dot(jnp.concatenate(_split2(sp), axis=-1), suffix2) for sp in sk] for sk in sps]
        ws = [[jnp.exp(zcs[k][j] - incl[k][j]).astype(BF16) for j in range(nh)] for k in range(len(kbs))]
        for k, st in enumerate(starts):
            vblock = v_ref[0, pl.ds(st, blk), :]
            accs = [accs[j] + _dot(ws[k][j], vblock) for j in range(nh)]
        return cs, accs

    carry = ([jnp.zeros((blk, 1), F32) for _ in range(nh)], [jnp.zeros((blk, lanes), F32) for _ in range(nh)])
    carry = blocks([qi], carry, col < row)
    odd = qi % 2
    carry = lax.fori_loop(0, odd, lambda i, cr: blocks([qi - 1], cr, None), carry)
    top = qi - odd
    _, accs = lax.fori_loop(0, top // 2, lambda i, cr: blocks([top - 1 - 2 * i, top - 2 - 2 * i], cr, None), carry)
    out = accs[0]
    for j in range(1, nh):
        out = jnp.where(lane_head == j, accs[j], out)
    o_ref[0] = out.astype(o_ref.dtype)


def _sb_prompt(q, k, v, bias, hd):
    bsz, seq, d = q.shape
    lanes = ATT_LANES
    blk = min(ATT_BLK, seq)
    assert seq % blk == 0 and d % lanes == 0 and lanes % hd == 0 and blk & (blk - 1) == 0
    return pl.pallas_call(
        functools.partial(_sb_prompt_kernel, blk=blk, hd=hd),
        grid=(bsz, d // lanes, seq // blk),
        in_specs=[pl.BlockSpec(memory_space=pltpu.SMEM),
                  pl.BlockSpec((1, blk, lanes), lambda b, h, i: (b, i, h)),
                  pl.BlockSpec((1, seq, lanes), lambda b, h, i: (b, 0, h)),
                  pl.BlockSpec((1, seq, lanes), lambda b, h, i: (b, 0, h))],
        out_specs=pl.BlockSpec((1, blk, lanes), lambda b, h, i: (b, i, h)),
        out_shape=jax.ShapeDtypeStruct((bsz, seq, d), BF16),
        compiler_params=_cparams(("parallel", "parallel", "arbitrary")),
        name="sb_prompt",
    )(bias, q, k, v)


def _sb_decode_kernel(pt_ref, bias_ref, q_ref, kn_ref, vn_ref, *rest, heads, pages_per_step, psize, nq):
    kp = rest[:pages_per_step]
    vp = rest[pages_per_step:2 * pages_per_step]
    o_ref = rest[2 * pages_per_step]
    qbd, acc, cbuf, kpad, vpad = rest[2 * pages_per_step + 1:]
    del pt_ref
    step = pl.program_id(1)
    d = q_ref.shape[-1]
    hd = d // heads
    rows = nq * heads
    lane_head = lax.broadcasted_iota(jnp.int32, (rows, d), 1) // hd
    row_head = lax.broadcasted_iota(jnp.int32, (rows, d), 0) & (heads - 1)
    own = lane_head == row_head
    suffix2 = _suffix2(psize)
    bias = bias_ref[...]

    def attend_new_rows(kblk, vblk, vis):
        z = _dot_nt(qbd[...], kblk) + bias
        w, c = _sb_weights(z, vis, cbuf[...], suffix2)
        acc[...] += _dot(w.astype(BF16), vblk)
        cbuf[...] = c

    @pl.when(step == 0)
    def _new_rows():
        qv = q_ref[0]
        qbd[...] = jnp.where(own, jnp.concatenate(
            [jnp.broadcast_to(qv[t:t + 1, :], (heads, d)) for t in range(nq)], axis=0), 0.0).astype(BF16)
        acc[...] = jnp.zeros_like(acc)
        cbuf[...] = jnp.zeros_like(cbuf)
        kpad[...] = jnp.zeros_like(kpad)
        vpad[...] = jnp.zeros_like(vpad)
        kpad[pl.ds(0, nq), :] = kn_ref[0]
        vpad[pl.ds(0, nq), :] = vn_ref[0]
        s_idx = lax.broadcasted_iota(jnp.int32, (rows, psize), 1)
        t_idx = lax.broadcasted_iota(jnp.int32, (rows, psize), 0) // heads
        attend_new_rows(kpad[...].astype(BF16), vpad[...].astype(BF16), s_idx < t_idx)

    q_bd = qbd[...]
    zs = [_dot(q_bd, kp[i][...].astype(BF16)) + bias for i in range(pages_per_step)]
    sps = [_softplus(z) for z in zs]
    c = cbuf[...]
    zcs = []
    for i in range(pages_per_step):
        zcs.append(zs[i] - c)
        c = c + jnp.sum(sps[i], axis=-1, keepdims=True)
    cbuf[...] = c
    incl = [_dot(jnp.concatenate(_split2(sp), axis=-1), suffix2) for sp in sps]
    ws = [jnp.exp(zcs[i] - incl[i]).astype(BF16) for i in range(pages_per_step)]
    out = _dot_nt(ws[0], vp[0][...].astype(BF16))
    for i in range(1, pages_per_step):
        out = out + _dot_nt(ws[i], vp[i][...].astype(BF16))
    acc[...] += out

    @pl.when(step == pl.num_programs(1) - 1)
    def _finish():
        a = jnp.where(own, acc[...], 0.0)
        o_ref[0] = jnp.concatenate(
            [jnp.sum(a[t * heads:(t + 1) * heads, :], axis=0, keepdims=True) for t in range(nq)],
            axis=0).astype(o_ref.dtype)


def _sb_decode(q, k_new, v_new, cache_k, cache_v, page_table, bias, layer, heads):
    bsz, nq, d = q.shape
    n_pages = page_table.shape[1]
    psize = cache_k.shape[3]
    pps = DECODE_PAGES_PER_STEP if n_pages % DECODE_PAGES_PER_STEP == 0 else 1
    steps = n_pages // pps
    rows = nq * heads
    assert heads & (heads - 1) == 0 and psize & (psize - 1) == 0 and cache_k.shape[2] == d

    def page_spec(i):
        def imap(b, s, pt):
            return (layer, pt[b, n_pages - 1 - (s * pps + i)], 0, 0)
        return pl.BlockSpec((None, None, d, psize), imap)

    tok = lambda b, s, pt: (b, 0, 0)
    bias_rows = jnp.tile(bias, nq)[:, None].astype(F32)
    grid_spec = pltpu.PrefetchScalarGridSpec(
        num_scalar_prefetch=1,
        grid=(bsz, steps),
        in_specs=[pl.BlockSpec((rows, 1), lambda b, s, pt: (0, 0)),
                  pl.BlockSpec((1, nq, d), tok),
                  pl.BlockSpec((1, nq, d), tok),
                  pl.BlockSpec((1, nq, d), tok)]
                 + [page_spec(i) for i in range(pps)] + [page_spec(i) for i in range(pps)],
        out_specs=pl.BlockSpec((1, nq, d), tok),
        scratch_shapes=[pltpu.VMEM((rows, d), BF16),
                        pltpu.VMEM((rows, d), F32),
                        pltpu.VMEM((rows, 1), F32),
                        pltpu.VMEM((psize, d), F32),
                        pltpu.VMEM((psize, d), F32)],
    )
    return pl.pallas_call(
        functools.partial(_sb_decode_kernel, heads=heads, pages_per_step=pps, psize=psize, nq=nq),
        grid_spec=grid_spec,
        out_shape=jax.ShapeDtypeStruct((bsz, nq, d), F32),
        compiler_params=_cparams(("parallel", "arbitrary")),
        name="sb_decode",
    )(page_table, bias_rows, q, k_new, v_new, *([cache_k] * pps), *([cache_v] * pps))


def _trunk(x, conv_state, hgrn_state, cache_k, cache_v, page_table, w, tm):
    bsz, seq, d = x.shape
    depth = w['norm_mix'].shape[0]
    heads = w['sb_bias'].shape[1]
    hd = d // heads
    xf = x.reshape(bsz * seq, d)
    new_conv, new_hgrn, new_k, new_v = [], [], [], []
    for l in range(depth):
        if l % 2 == 0:
            e = l // 2
            proj = _norm_matmul(xf, w['norm_mix'][l][None, :], w['w_in_even'][e], tm)
            ab, cso, hso = _even_mixer(proj.reshape(bsz, seq, -1), conv_state[e], hgrn_state[e],
                                       w['hgrn_lb_logits'], w['conv_w'][e], w['conv_b'][e],
                                       w['conv_ln_g'][e], w['conv_ln_b'][e], w['hgrn_norm_g'][e], e)
            new_conv.append(cso)
            new_hgrn.append(hso)
            mix_in, w_mix = ab.reshape(bsz * seq, d), w['w_out_even'][e]
        else:
            a = l // 2
            q, k, v, kb, vb = _qkv_proj(xf, w['norm_mix'][l][None, :], w['w_qkv'][a],
                                        w['q_norm_g'][a], w['k_norm_g'][a], heads, tm)
            if cache_k is None:
                o = _sb_prompt(q.reshape(bsz, seq, d), kb.reshape(bsz, seq, d), vb.reshape(bsz, seq, d),
                               w['sb_bias'][a], hd)
            else:
                as_rows = lambda t: t.astype(F32).reshape(bsz, seq, d)
                o = _sb_decode(as_rows(q), as_rows(kb), as_rows(vb),
                               cache_k, cache_v, page_table, w['sb_bias'][a], a, heads)
            new_k.append(k.reshape(bsz, seq, heads, hd))
            new_v.append(v.reshape(bsz, seq, heads, hd))
            mix_in, w_mix = o.reshape(bsz * seq, d), w['w_o'][a]
        xf = _mix_ffn(xf, mix_in, w_mix, w['norm_ffn'][l][None, :], w['w_up'][l], w['w_down'][l], tm)
    return (xf.reshape(bsz, seq, d), jnp.stack(new_conv), jnp.stack(new_hgrn),
            jnp.stack(new_k), jnp.stack(new_v))


def kernel(x_prompt, x_sample, state_conv, state_hgrn, cache_k, cache_v, page_table, norm_mix, norm_ffn, w_in_even, conv_w, conv_b, conv_ln_g, conv_ln_b, hgrn_lb_logits, hgrn_norm_g, w_out_even, w_qkv, q_norm_g, k_norm_g, sb_bias, w_o, w_up, w_down):
    w = dict(norm_mix=norm_mix, norm_ffn=norm_ffn, conv_w=conv_w, conv_b=conv_b, conv_ln_g=conv_ln_g,
             conv_ln_b=conv_ln_b, hgrn_lb_logits=hgrn_lb_logits, hgrn_norm_g=hgrn_norm_g,
             q_norm_g=q_norm_g, k_norm_g=k_norm_g, sb_bias=sb_bias)
    for name, t in dict(w_in_even=w_in_even, w_out_even=w_out_even, w_qkv=w_qkv, w_o=w_o,
                        w_up=w_up, w_down=w_down).items():
        w[name] = t.astype(BF16)
    bsz, seq, d = x_prompt.shape
    n_mix = state_conv.shape[0]
    conv0 = jnp.zeros((n_mix, bsz) + state_conv.shape[2:], F32)
    hgrn0 = jnp.zeros((n_mix, bsz) + state_hgrn.shape[2:], F32)
    tm_p = 256 if (bsz * seq) % 256 == 0 else bsz * seq
    y_p, conv_p, hgrn_p, k_p, v_p = _trunk(x_prompt, conv0, hgrn0, None, None, None, w, tm_p)
    sb, ss, _ = x_sample.shape
    lay, pages, psize = cache_k.shape[:3]
    ck = jnp.transpose(cache_k, (0, 1, 3, 4, 2)).reshape(lay, pages, d, psize)
    cv = jnp.transpose(cache_v, (0, 1, 3, 4, 2)).reshape(lay, pages, d, psize)
    y_s, conv_s, hgrn_s, k_s, v_s = _trunk(x_sample, state_conv, state_hgrn, ck, cv, page_table, w, sb * ss)
    return (y_p, y_s, conv_p, conv_s, hgrn_p, hgrn_s, k_p, v_p, k_s, v_s)
```

```python
import functools

import jax
import jax.numpy as jnp
from jax import lax
from jax.experimental import pallas as pl
from jax.experimental.pallas import tpu as pltpu

F32 = jnp.float32
BF16 = jnp.bfloat16

EPS = 1e-6
HGRN_CHUNK = 128
HGRN_BASE = 16
CONV_HALO = 32
ATT_BLK = 256
ATT_LANES = 256
ATT_GROUP = 2
DECODE_PAGES_PER_STEP = 16
VMEM_LIMIT = 56 * 1024 * 1024
MXU_TILE = 256
SUBLANES = 8
MASKED_LOGW = -1e30

def _cparams(sem, flags=None):
    return pltpu.CompilerParams(dimension_semantics=sem, vmem_limit_bytes=VMEM_LIMIT, flags=flags)


def _rms(x, g):
    ms = jnp.mean(x * x, axis=-1, keepdims=True)
    return x * lax.rsqrt(ms + EPS) * g


def _dot(a, b):
    return jnp.dot(a, b, preferred_element_type=F32)


def _dot_nt(a, b):
    return lax.dot_general(a, b, (((1,), (1,)), ((), ())), preferred_element_type=F32)


def _split2(x):
    hi = x.astype(BF16)
    lo = (x - hi.astype(F32)).astype(BF16)
    return hi, lo


def _split3(x):
    hi = x.astype(BF16)
    r = x - hi.astype(F32)
    mid = r.astype(BF16)
    lo = (r - mid.astype(F32)).astype(BF16)
    return hi, mid, lo


def _norm_matmul_kernel(x_ref, g_ref, w_ref, o_ref):
    h = _rms(x_ref[...], g_ref[...]).astype(BF16)
    o_ref[...] = _dot(h, w_ref[...])


def _norm_matmul(x, g, w, tm):
    m, d = x.shape
    n = w.shape[1]
    return pl.pallas_call(
        _norm_matmul_kernel,
        grid=(m // tm,),
        in_specs=[pl.BlockSpec((tm, d), lambda i: (i, 0)),
                  pl.BlockSpec((1, d), lambda i: (0, 0)),
                  pl.BlockSpec((d, n), lambda i: (0, 0))],
        out_specs=pl.BlockSpec((tm, n), lambda i: (i, 0)),
        out_shape=jax.ShapeDtypeStruct((m, n), F32),
        compiler_params=_cparams(("parallel",)),
        name="norm_matmul",
    )(x, g, w)


def _qkv_kernel(x_ref, g_ref, w_ref, qg_ref, kg_ref, q_ref, k_ref, v_ref, kb_ref, vb_ref,
                *, d, head_dim, q_scale):
    h = _rms(x_ref[...], g_ref[...]).astype(BF16)
    grp = MXU_TILE
    ri = lax.broadcasted_iota(jnp.int32, (2 * grp, grp), 0) & (grp - 1)
    ci = lax.broadcasted_iota(jnp.int32, (2 * grp, grp), 1)
    same_head2 = jnp.where(ri // head_dim == ci // head_dim, 1.0, 0.0).astype(BF16)

    def head_norm(t, gain):
        t2 = t * t
        parts = []
        for c in range(d // grp):
            hi, lo = _split2(t2[:, c * grp:(c + 1) * grp])
            parts.append(_dot(jnp.concatenate([hi, lo], axis=-1), same_head2))
        ss = jnp.concatenate(parts, axis=-1)
        return t * lax.rsqrt(ss * (1.0 / head_dim) + EPS) * gain

    q = head_norm(_dot(h, w_ref[:, 0:d]), qg_ref[...])
    q_ref[...] = (q * q_scale).astype(BF16)
    k = head_norm(_dot(h, w_ref[:, d:2 * d]), kg_ref[...])
    k_ref[...] = k.reshape(k_ref.shape)
    kb_ref[...] = k.astype(BF16)
    v = _dot(h, w_ref[:, 2 * d:3 * d])
    v_ref[...] = v.reshape(v_ref.shape)
    vb_ref[...] = v.astype(BF16)


def _qkv_proj(x, g, w, qg, kg, heads, tm):
    m, d = x.shape
    head_dim = d // heads
    assert d % MXU_TILE == 0 and MXU_TILE % head_dim == 0
    row = lambda i: (i, 0)
    fixed = lambda i: (0, 0)
    outs = pl.pallas_call(
        functools.partial(_qkv_kernel, d=d, head_dim=head_dim, q_scale=float(head_dim) ** -0.5),
        grid=(m // tm,),
        in_specs=[pl.BlockSpec((tm, d), row),
                  pl.BlockSpec((1, d), fixed),
                  pl.BlockSpec((d, 3 * d), fixed),
                  pl.BlockSpec((1, d), fixed),
                  pl.BlockSpec((1, d), fixed)],
        out_specs=[pl.BlockSpec((tm, d), row),
                   pl.BlockSpec((tm, heads, head_dim), lambda i: (i, 0, 0)),
                   pl.BlockSpec((tm, heads, head_dim), lambda i: (i, 0, 0)),
                   pl.BlockSpec((tm, d), row),
                   pl.BlockSpec((tm, d), row)],
        out_shape=[jax.ShapeDtypeStruct((m, d), BF16),
                   jax.ShapeDtypeStruct((m, heads, head_dim), F32),
                   jax.ShapeDtypeStruct((m, heads, head_dim), F32),
                   jax.ShapeDtypeStruct((m, d), BF16),
                   jax.ShapeDtypeStruct((m, d), BF16)],
        compiler_params=_cparams(("parallel",)),
        name="qkv_proj",
    )(x, g, w, jnp.tile(qg, heads)[None, :], jnp.tile(kg, heads)[None, :])
    return outs


def _mix_ffn_kernel(x_ref, m_ref, wo_ref, g_ref, wu_ref, wd_ref, o_ref, *, ff_chunk):
    x1 = x_ref[...] + _dot(m_ref[...].astype(BF16), wo_ref[...])
    h = _rms(x1, g_ref[...]).astype(BF16)
    acc = x1
    for c in range(wu_ref.shape[1] // ff_chunk):
        u = _dot(h, wu_ref[:, c * ff_chunk:(c + 1) * ff_chunk])
        u = jnp.square(jnp.maximum(u, 0.0)).astype(BF16)
        acc = acc + _dot(u, wd_ref[c * ff_chunk:(c + 1) * ff_chunk, :])
    o_ref[...] = acc


def _mix_ffn(x, mix_in, wo, g, wu, wd, tm):
    m, d = x.shape
    dff = wu.shape[1]
    row = lambda i: (i, 0)
    fixed = lambda i: (0, 0)
    return pl.pallas_call(
        functools.partial(_mix_ffn_kernel, ff_chunk=min(dff, 1024)),
        grid=(m // tm,),
        in_specs=[pl.BlockSpec((tm, d), row),
                  pl.BlockSpec((tm, d), row),
                  pl.BlockSpec((d, d), fixed),
                  pl.BlockSpec((1, d), fixed),
                  pl.BlockSpec((d, dff), fixed),
                  pl.BlockSpec((dff, d), fixed)],
        out_specs=pl.BlockSpec((tm, d), row),
        out_shape=jax.ShapeDtypeStruct((m, d), F32),
        compiler_params=_cparams(("parallel",)),
        name="mix_ffn",
    )(x, mix_in, wo, g, wu, wd)


def _sigmoid_pair(z):
    e = jnp.exp(-jnp.abs(z))
    inv = 1.0 / (1.0 + e)
    pos = z >= 0
    return jnp.where(pos, 1.0, e) * inv, jnp.where(pos, e, 1.0) * inv


def _even_mixer_kernel(p_ref, cs_ref, hs_ref, lbl_ref, cw_ref, cb_ref, lng_ref, lnb_ref, ng_ref,
                       ab_ref, cso_ref, hso_ref, ubuf, st, pbuf, aligned,
                       *, layer, rows, dc, heads, dk, taps):
    C = HGRN_CHUNK
    hk = heads * dk
    t_idx = pl.program_id(1)
    hist = taps - 1

    @pl.when(t_idx == 0)
    def _init():
        ubuf[...] = jnp.zeros_like(ubuf)
        ubuf[pl.ds(CONV_HALO - hist, hist), :] = cs_ref[0]
        for h in range(heads):
            st[h] = hs_ref[0, h].T

    if rows < C:
        pbuf[...] = jnp.zeros_like(pbuf)
        pbuf[pl.ds(0, rows), :] = p_ref[0]
        src = pbuf
    else:
        src = p_ref.at[0]

    val = src[:, 0:dc]
    gate = src[:, dc:2 * dc]
    u = val * _sigmoid_pair(gate)[0]
    ubuf[pl.ds(CONV_HALO, C), :] = u
    cw = cw_ref[...]
    y = jnp.zeros((C, dc), F32) + cb_ref[...]
    first = CONV_HALO - hist
    for phase in range(SUBLANES):
        offs = [first + j for j in range(taps) if (first + j) % SUBLANES == phase]
        if not offs:
            continue
        n = offs[-1] - offs[0] + C
        aligned[pl.ds(0, n), :] = ubuf[pl.ds(offs[0], n), :]
        for o in offs:
            y = y + cw[o - first:o - first + 1, :] * aligned[pl.ds(o - offs[0], C), :]
    mu = jnp.mean(y, axis=-1, keepdims=True)
    yc = y - mu
    var = jnp.mean(yc * yc, axis=-1, keepdims=True)
    y = yc * lax.rsqrt(var + EPS) * lng_ref[...] + lnb_ref[...]
    a_out = y * _sigmoid_pair(y)[0]

    @pl.when(t_idx == pl.num_programs(1) - 1)
    def _conv_state_out():
        cso_ref[0] = ubuf[pl.ds(CONV_HALO + rows - hist, hist), :]

    ubuf[pl.ds(0, CONV_HALO), :] = ubuf[pl.ds(C, CONV_HALO), :]

    lbl = lbl_ref[...]
    lmax = jnp.max(lbl, axis=0, keepdims=True)
    pe = jnp.exp(lbl - lmax)
    psm = pe / jnp.sum(pe, axis=0, keepdims=True)
    lb = jnp.zeros((1, hk), F32)
    for i in range(1, layer + 1):
        lb = lb + psm[i:i + 1, :]

    o0 = 2 * dc
    q = src[:, o0:o0 + hk]
    z = src[:, o0 + hk:o0 + 2 * hk]
    vin = src[:, o0 + 2 * hk:o0 + 3 * hk]
    og = src[:, o0 + 3 * hk:o0 + 4 * hk]
    sig, nsig = _sigmoid_pair(z)
    f = lb + (1.0 - lb) * sig
    k = (1.0 - lb) * nsig
    logf = jnp.log(f)
    row = lax.broadcasted_iota(jnp.int32, (C, hk), 0)
    if rows < C:
        valid = row < rows
        logf = jnp.where(valid, logf, 0.0)
        k = jnp.where(valid, k, 0.0)
    ri = lax.broadcasted_iota(jnp.int32, (C, C), 0)
    ci = lax.broadcasted_iota(jnp.int32, (C, C), 1)
    tril = jnp.where(ri >= ci, 1.0, 0.0).astype(BF16)
    hi, mid, lo = _split3(logf)
    b = _dot(tril, hi) + _dot(tril, mid) + _dot(tril, lo)

    def row_bcast(r, n):
        return jnp.broadcast_to(b[r:r + 1, :], (n, hk))

    b_end = b[C - 1:C, :]
    q_inter = (q * jnp.exp(b)).astype(BF16)
    k_end = (k * jnp.exp(b_end - b)).astype(BF16)
    dec = jnp.exp(b_end)
    levels = []
    m = C // 2
    while m >= HGRN_BASE:
        ref = jnp.concatenate([row_bcast(blk * 2 * m + m - 1, 2 * m) for blk in range(C // (2 * m))], axis=0)
        is_q = (row & (2 * m - 1)) >= m
        e = jnp.where(is_q, b - ref, ref - b)
        levels.append((m, (jnp.where(is_q, q, k) * jnp.exp(e)).astype(BF16)))
        m //= 2
    base = jnp.concatenate(
        [jnp.zeros((HGRN_BASE, hk), F32)]
        + [row_bcast(blk * HGRN_BASE - 1, HGRN_BASE) for blk in range(1, C // HGRN_BASE)], axis=0)
    bl = b - base
    q_diag = (q * jnp.exp(bl)).astype(BF16)
    k_diag = (k * jnp.exp(-bl)).astype(BF16)
    xor = ri ^ ci
    causal = ci <= ri
    vb = vin.astype(BF16)

    outs = []
    for h in range(heads):
        sl = slice(h * dk, (h + 1) * dk)
        att = _dot_nt(q_diag[:, sl], k_diag[:, sl])
        for m, x in reversed(levels):
            att = jnp.where(xor < m, att, _dot_nt(x[:, sl], x[:, sl]))
        att = jnp.where(causal, att, 0.0)
        s_t = st[h]
        o = _dot(att.astype(BF16), vb[:, sl]) + _dot_nt(q_inter[:, sl], s_t.astype(BF16))
        st[h] = s_t * dec[:, sl] + _dot(vin[:, sl].T.astype(BF16), k_end[:, sl])
        o = _rms(o, ng_ref[...])
        g = og[:, sl]
        outs.append(o * (g * _sigmoid_pair(g)[0]))
    b_out = jnp.concatenate(outs, axis=-1)
    ab = jnp.concatenate([a_out, b_out], axis=-1)
    ab_ref[0] = ab[0:rows, :].astype(ab_ref.dtype)

    @pl.when(t_idx == pl.num_programs(1) - 1)
    def _hgrn_state_out():
        for h in range(heads):
            hso_ref[0, h] = st[h].T


def _even_mixer(proj, conv_state, hgrn_state, lb_logits, cw, cb, lng, lnb, ng, layer):
    bsz, seq, width = proj.shape
    taps, dc = cw.shape
    heads, dk, dv = hgrn_state.shape[1:]
    assert dk == dv and width == 2 * dc + 4 * heads * dk
    rows = min(seq, HGRN_CHUNK)
    assert seq % rows == 0
    nt = seq // rows
    kern = functools.partial(_even_mixer_kernel, layer=layer, rows=rows, dc=dc, heads=heads, dk=dk, taps=taps)
    fixed2 = lambda b, t: (0, 0)
    return pl.pallas_call(
        kern,
        grid=(bsz, nt),
        in_specs=[pl.BlockSpec((1, rows, width), lambda b, t: (b, t, 0)),
                  pl.BlockSpec((1, taps - 1, dc), lambda b, t: (b, 0, 0)),
                  pl.BlockSpec((1, heads, dk, dv), lambda b, t: (b, 0, 0, 0)),
                  pl.BlockSpec(lb_logits.shape, fixed2),
                  pl.BlockSpec((taps, dc), fixed2),
                  pl.BlockSpec((1, dc), fixed2),
                  pl.BlockSpec((1, dc), fixed2),
                  pl.BlockSpec((1, dc), fixed2),
                  pl.BlockSpec((1, dv), fixed2)],
        out_specs=[pl.BlockSpec((1, rows, dc + heads * dv), lambda b, t: (b, t, 0)),
                   pl.BlockSpec((1, taps - 1, dc), lambda b, t: (b, 0, 0)),
                   pl.BlockSpec((1, heads, dk, dv), lambda b, t: (b, 0, 0, 0))],
        out_shape=[jax.ShapeDtypeStruct((bsz, seq, dc + heads * dv), BF16 if rows % 16 == 0 else F32),
                   jax.ShapeDtypeStruct((bsz, taps - 1, dc), F32),
                   jax.ShapeDtypeStruct((bsz, heads, dk, dv), F32)],
        scratch_shapes=[pltpu.VMEM((CONV_HALO + HGRN_CHUNK + CONV_HALO, dc), F32),
                        pltpu.VMEM((heads, dv, dk), F32),
                        pltpu.VMEM((HGRN_CHUNK, width), F32),
                        pltpu.VMEM((HGRN_CHUNK + CONV_HALO, dc), F32)],
        compiler_params=_cparams(("parallel", "arbitrary")),
        name="even_mixer",
    )(proj, conv_state, hgrn_state, lb_logits, cw, cb[None, :], lng[None, :], lnb[None, :], ng[None, :])


def _softplus(z):
    return jnp.maximum(z, 0.0) + jnp.log(1.0 + jnp.exp(-jnp.abs(z)))


def _sb_weights(z, vis, carry, suffix2):
    sp = _softplus(z)
    if vis is not None:
        sp = jnp.where(vis, sp, 0.0)
    incl = _dot(jnp.concatenate(_split2(sp), axis=-1), suffix2)
    w = jnp.exp(z - carry - incl)
    if vis is not None:
        w = jnp.where(vis, w, 0.0)
    return w, carry + jnp.sum(sp, axis=-1, keepdims=True)


def _suffix2(n):
    j = lax.broadcasted_iota(jnp.int32, (2 * n, n), 0) & (n - 1)
    s = lax.broadcasted_iota(jnp.int32, (2 * n, n), 1)
    return jnp.where(j >= s, 1.0, 0.0).astype(BF16)


def _sb_prompt_kernel(bias_ref, q_ref, k_ref, v_ref, o_ref, vm, *, blk, hd):
    lanes = q_ref.shape[-1]
    nh = lanes // hd
    hp = pl.program_id(1)
    qi = pl.program_id(2)
    lane_head = lax.broadcasted_iota(jnp.int32, (blk, lanes), 1) // hd

    @pl.when(qi == 0)
    def _mask_values():
        def fill(r, _):
            start = pl.multiple_of(r * blk, blk)
            vf = v_ref[0, pl.ds(start, blk), :].astype(F32)
            for j in range(nh):
                vm[j, pl.ds(start, blk), :] = jnp.where(lane_head == j, vf, 0.0).astype(BF16)
            return 0
        lax.fori_loop(0, v_ref.shape[1] // blk, fill, 0)

    qf = q_ref[0].astype(F32)
    qs = [jnp.where(lane_head == j, qf, 0.0).astype(BF16) for j in range(nh)]
    biases = [bias_ref[hp * nh + j] for j in range(nh)]
    suffix2 = _suffix2(blk)
    row = lax.broadcasted_iota(jnp.int32, (blk, blk), 0)
    col = lax.broadcasted_iota(jnp.int32, (blk, blk), 1)

    def blocks(kbs, carry, vis):
        cs, acc = carry
        starts = [pl.multiple_of(kb * blk, blk) for kb in kbs]
        zs = [[_dot_nt(qs[j], k_ref[0, pl.ds(st, blk), :]) + biases[j] for j in range(nh)] for st in starts]
        sps = [[_softplus(z) for z in zk] for zk in zs]
        if vis is not None:
            sps[0] = [jnp.where(vis, sp, 0.0) for sp in sps[0]]
        sums = [[jnp.sum(sp, axis=-1, keepdims=True) for sp in sk] for sk in sps]
        zcs = []
        for k in range(len(kbs)):
            zcs.append([zs[k][j] - cs[j] for j in range(nh)])
            cs = [cs[j] + sums[k][j] for j in range(nh)]
        if vis is not None:
            zcs[0] = [jnp.where(vis, zc, MASKED_LOGW) for zc in zcs[0]]
        incl = [[_dot(jnp.concatenate(_split2(sp), axis=-1), suffix2) for sp in sk] for sk in sps]
        ws = [[jnp.exp(zcs[k][j] - incl[k][j]).astype(BF16) for j in range(nh)] for k in range(len(kbs))]
        w_all = jnp.concatenate([ws[k][j] for k in range(len(kbs)) for j in range(nh)], axis=1)
        v_all = jnp.concatenate([vm[j, pl.ds(st, blk), :] for st in starts for j in range(nh)], axis=0)
        return cs, acc + _dot(w_all, v_all)

    carry = ([jnp.zeros((blk, 1), F32) for _ in range(nh)], jnp.zeros((blk, lanes), F32))
    carry = blocks([qi], carry, col < row)
    rem = qi % ATT_GROUP
    carry = lax.fori_loop(0, rem, lambda i, cr: blocks([qi - 1 - i], cr, None), carry)
    top = qi - rem
    _, acc = lax.fori_loop(
        0, top // ATT_GROUP,
        lambda i, cr: blocks([top - 1 - ATT_GROUP * i - g for g in range(ATT_GROUP)], cr, None), carry)
    o_ref[0] = acc.astype(o_ref.dtype)


def _sb_prompt(q, k, v, bias, hd):
    bsz, seq, d = q.shape
    lanes = ATT_LANES
    blk = min(ATT_BLK, seq)
    assert seq % blk == 0 and d % lanes == 0 and lanes % hd == 0 and blk & (blk - 1) == 0
    return pl.pallas_call(
        functools.partial(_sb_prompt_kernel, blk=blk, hd=hd),
        grid=(bsz, d // lanes, seq // blk),
        in_specs=[pl.BlockSpec(memory_space=pltpu.SMEM),
                  pl.BlockSpec((1, blk, lanes), lambda b, h, i: (b, i, h)),
                  pl.BlockSpec((1, seq, lanes), lambda b, h, i: (b, 0, h)),
                  pl.BlockSpec((1, seq, lanes), lambda b, h, i: (b, 0, h))],
        out_specs=pl.BlockSpec((1, blk, lanes), lambda b, h, i: (b, i, h)),
        out_shape=jax.ShapeDtypeStruct((bsz, seq, d), BF16),
        scratch_shapes=[pltpu.VMEM((lanes // hd, seq, lanes), BF16)],
        compiler_params=_cparams(("parallel", "parallel", "arbitrary")),
        name="sb_prompt",
    )(bias, q, k, v)


def _sb_decode_kernel(pt_ref, bias_ref, q_ref, kn_ref, vn_ref, *rest, heads, pages_per_step, psize, nq):
    kp = rest[:pages_per_step]
    vp = rest[pages_per_step:2 * pages_per_step]
    o_ref = rest[2 * pages_per_step]
    qbd, acc, cbuf, kpad, vpad = rest[2 * pages_per_step + 1:]
    del pt_ref
    step = pl.program_id(1)
    d = q_ref.shape[-1]
    hd = d // heads
    rows = nq * heads
    lane_head = lax.broadcasted_iota(jnp.int32, (rows, d), 1) // hd
    row_head = lax.broadcasted_iota(jnp.int32, (rows, d), 0) & (heads - 1)
    own = lane_head == row_head
    suffix2 = _suffix2(psize)
    bias = bias_ref[...]

    def attend_new_rows(kblk, vblk, vis):
        z = _dot_nt(qbd[...], kblk) + bias
        w, c = _sb_weights(z, vis, cbuf[...], suffix2)
        acc[...] += _dot(w.astype(BF16), vblk)
        cbuf[...] = c

    @pl.when(step == 0)
    def _new_rows():
        qv = q_ref[0]
        qbd[...] = jnp.where(own, jnp.concatenate(
            [jnp.broadcast_to(qv[t:t + 1, :], (heads, d)) for t in range(nq)], axis=0), 0.0).astype(BF16)
        acc[...] = jnp.zeros_like(acc)
        cbuf[...] = jnp.zeros_like(cbuf)
        kpad[...] = jnp.zeros_like(kpad)
        vpad[...] = jnp.zeros_like(vpad)
        kpad[pl.ds(0, nq), :] = kn_ref[0]
        vpad[pl.ds(0, nq), :] = vn_ref[0]
        s_idx = lax.broadcasted_iota(jnp.int32, (rows, psize), 1)
        t_idx = lax.broadcasted_iota(jnp.int32, (rows, psize), 0) // heads
        attend_new_rows(kpad[...].astype(BF16), vpad[...].astype(BF16), s_idx < t_idx)

    q_bd = qbd[...]
    zs = [_dot(q_bd, kp[i][...].astype(BF16)) + bias for i in range(pages_per_step)]
    sps = [_softplus(z) for z in zs]
    c = cbuf[...]
    zcs = []
    for i in range(pages_per_step):
        zcs.append(zs[i] - c)
        c = c + jnp.sum(sps[i], axis=-1, keepdims=True)
    cbuf[...] = c
    incl = [_dot(jnp.concatenate(_split2(sp), axis=-1), suffix2) for sp in sps]
    ws = [jnp.exp(zcs[i] - incl[i]).astype(BF16) for i in range(pages_per_step)]
    out = _dot_nt(ws[0], vp[0][...].astype(BF16))
    for i in range(1, pages_per_step):
        out = out + _dot_nt(ws[i], vp[i][...].astype(BF16))
    acc[...] += out

    @pl.when(step == pl.num_programs(1) - 1)
    def _finish():
        a = jnp.where(own, acc[...], 0.0)
        o_ref[0] = jnp.concatenate(
            [jnp.sum(a[t * heads:(t + 1) * heads, :], axis=0, keepdims=True) for t in range(nq)],
            axis=0).astype(o_ref.dtype)


def _sb_decode(q, k_new, v_new, cache_k, cache_v, page_table, bias, layer, heads):
    bsz, nq, d = q.shape
    n_pages = page_table.shape[1]
    psize = cache_k.shape[3]
    pps = DECODE_PAGES_PER_STEP if n_pages % DECODE_PAGES_PER_STEP == 0 else 1
    steps = n_pages // pps
    rows = nq * heads
    assert heads & (heads - 1) == 0 and psize & (psize - 1) == 0 and cache_k.shape[2] == d

    def page_spec(i):
        def imap(b, s, pt):
            return (layer, pt[b, n_pages - 1 - (s * pps + i)], 0, 0)
        return pl.BlockSpec((None, None, d, psize), imap)

    tok = lambda b, s, pt: (b, 0, 0)
    bias_rows = jnp.tile(bias, nq)[:, None].astype(F32)
    grid_spec = pltpu.PrefetchScalarGridSpec(
        num_scalar_prefetch=1,
        grid=(bsz, steps),
        in_specs=[pl.BlockSpec((rows, 1), lambda b, s, pt: (0, 0)),
                  pl.BlockSpec((1, nq, d), tok),
                  pl.BlockSpec((1, nq, d), tok),
                  pl.BlockSpec((1, nq, d), tok)]
                 + [page_spec(i) for i in range(pps)] + [page_spec(i) for i in range(pps)],
        out_specs=pl.BlockSpec((1, nq, d), tok),
        scratch_shapes=[pltpu.VMEM((rows, d), BF16),
                        pltpu.VMEM((rows, d), F32),
                        pltpu.VMEM((rows, 1), F32),
                        pltpu.VMEM((psize, d), F32),
                        pltpu.VMEM((psize, d), F32)],
    )
    return pl.pallas_call(
        functools.partial(_sb_decode_kernel, heads=heads, pages_per_step=pps, psize=psize, nq=nq),
        grid_spec=grid_spec,
        out_shape=jax.ShapeDtypeStruct((bsz, nq, d), F32),
        compiler_params=_cparams(("parallel", "arbitrary")),
        name="sb_decode",
    )(page_table, bias_rows, q, k_new, v_new, *([cache_k] * pps), *([cache_v] * pps))


def _trunk(x, conv_state, hgrn_state, cache_k, cache_v, page_table, w, tm):
    bsz, seq, d = x.shape
    depth = w['norm_mix'].shape[0]
    heads = w['sb_bias'].shape[1]
    hd = d // heads
    xf = x.reshape(bsz * seq, d)
    new_conv, new_hgrn, new_k, new_v = [], [], [], []
    for l in range(depth):
        if l % 2 == 0:
            e = l // 2
            proj = _norm_matmul(xf, w['norm_mix'][l][None, :], w['w_in_even'][e], tm)
            ab, cso, hso = _even_mixer(proj.reshape(bsz, seq, -1), conv_state[e], hgrn_state[e],
                                       w['hgrn_lb_logits'], w['conv_w'][e], w['conv_b'][e],
                                       w['conv_ln_g'][e], w['conv_ln_b'][e], w['hgrn_norm_g'][e], e)
            new_conv.append(cso)
            new_hgrn.append(hso)
            mix_in, w_mix = ab.reshape(bsz * seq, d), w['w_out_even'][e]
        else:
            a = l // 2
            q, k, v, kb, vb = _qkv_proj(xf, w['norm_mix'][l][None, :], w['w_qkv'][a],
                                        w['q_norm_g'][a], w['k_norm_g'][a], heads, tm)
            if cache_k is None:
                o = _sb_prompt(q.reshape(bsz, seq, d), kb.reshape(bsz, seq, d), vb.reshape(bsz, seq, d),
                               w['sb_bias'][a], hd)
            else:
                as_rows = lambda t: t.astype(F32).reshape(bsz, seq, d)
                o = _sb_decode(as_rows(q), as_rows(kb), as_rows(vb),
                               cache_k, cache_v, page_table, w['sb_bias'][a], a, heads)
            new_k.append(k.reshape(bsz, seq, heads, hd))
            new_v.append(v.reshape(bsz, seq, heads, hd))
            mix_in, w_mix = o.reshape(bsz * seq, d), w['w_o'][a]
        xf = _mix_ffn(xf, mix_in, w_mix, w['norm_ffn'][l][None, :], w['w_up'][l], w['w_down'][l], tm)
    return (xf.reshape(bsz, seq, d), jnp.stack(new_conv), jnp.stack(new_hgrn),
            jnp.stack(new_k), jnp.stack(new_v))


def kernel(x_prompt, x_sample, state_conv, state_hgrn, cache_k, cache_v, page_table, norm_mix, norm_ffn, w_in_even, conv_w, conv_b, conv_ln_g, conv_ln_b, hgrn_lb_logits, hgrn_norm_g, w_out_even, w_qkv, q_norm_g, k_norm_g, sb_bias, w_o, w_up, w_down):
    w = dict(norm_mix=norm_mix, norm_ffn=norm_ffn, conv_w=conv_w, conv_b=conv_b, conv_ln_g=conv_ln_g,
             conv_ln_b=conv_ln_b, hgrn_lb_logits=hgrn_lb_logits, hgrn_norm_g=hgrn_norm_g,
             q_norm_g=q_norm_g, k_norm_g=k_norm_g, sb_bias=sb_bias)
    for name, t in dict(w_in_even=w_in_even, w_out_even=w_out_even, w_qkv=w_qkv, w_o=w_o,
                        w_up=w_up, w_down=w_down).items():
        w[name] = t.astype(BF16)
    bsz, seq, d = x_prompt.shape
    n_mix = state_conv.shape[0]
    conv0 = jnp.zeros((n_mix, bsz) + state_conv.shape[2:], F32)
    hgrn0 = jnp.zeros((n_mix, bsz) + state_hgrn.shape[2:], F32)
    tm_p = 256 if (bsz * seq) % 256 == 0 else bsz * seq
    y_p, conv_p, hgrn_p, k_p, v_p = _trunk(x_prompt, conv0, hgrn0, None, None, None, w, tm_p)
    sb, ss, _ = x_sample.shape
    lay, pages, psize = cache_k.shape[:3]
    ck = jnp.transpose(cache_k, (0, 1, 3, 4, 2)).reshape(lay, pages, d, psize)
    cv = jnp.transpose(cache_v, (0, 1, 3, 4, 2)).reshape(lay, pages, d, psize)
    y_s, conv_s, hgrn_s, k_s, v_s = _trunk(x_sample, state_conv, state_hgrn, ck, cv, page_table, w, sb * ss)
    return (y_p, y_s, conv_p, conv_s, hgrn_p, hgrn_s, k_p, v_p, k_s, v_s)
```

```python
import functools

import jax
import jax.numpy as jnp
from jax import lax
from jax.experimental import pallas as pl
from jax.experimental.pallas import tpu as pltpu

F32 = jnp.float32
BF16 = jnp.bfloat16

EPS = 1e-6
HGRN_CHUNK = 128
HGRN_BASE = 16
CONV_HALO = 32
ATT_BLK = 256
ATT_LANES = 256
ATT_GROUP = 2
DECODE_PAGES_PER_STEP = 16
VMEM_LIMIT = 56 * 1024 * 1024
MXU_TILE = 256
SUBLANES = 8
MASKED_LOGW = -1e30

def _cparams(sem, flags=None):
    return pltpu.CompilerParams(dimension_semantics=sem, vmem_limit_bytes=VMEM_LIMIT, flags=flags)


def _rms(x, g):
    ms = jnp.mean(x * x, axis=-1, keepdims=True)
    return x * lax.rsqrt(ms + EPS) * g


def _dot(a, b):
    return jnp.dot(a, b, preferred_element_type=F32)


def _dot_nt(a, b):
    return lax.dot_general(a, b, (((1,), (1,)), ((), ())), preferred_element_type=F32)


def _split2(x):
    hi = x.astype(BF16)
    lo = (x - hi.astype(F32)).astype(BF16)
    return hi, lo


def _split3(x):
    hi = x.astype(BF16)
    r = x - hi.astype(F32)
    mid = r.astype(BF16)
    lo = (r - mid.astype(F32)).astype(BF16)
    return hi, mid, lo


def _norm_matmul_kernel(x_ref, g_ref, w_ref, o_ref):
    h = _rms(x_ref[...], g_ref[...]).astype(BF16)
    o_ref[...] = _dot(h, w_ref[...])


def _norm_matmul(x, g, w, tm):
    m, d = x.shape
    n = w.shape[1]
    return pl.pallas_call(
        _norm_matmul_kernel,
        grid=(m // tm,),
        in_specs=[pl.BlockSpec((tm, d), lambda i: (i, 0)),
                  pl.BlockSpec((1, d), lambda i: (0, 0)),
                  pl.BlockSpec((d, n), lambda i: (0, 0))],
        out_specs=pl.BlockSpec((tm, n), lambda i: (i, 0)),
        out_shape=jax.ShapeDtypeStruct((m, n), F32),
        compiler_params=_cparams(("parallel",)),
        name="norm_matmul",
    )(x, g, w)


def _qkv_kernel(x_ref, g_ref, w_ref, qg_ref, kg_ref, q_ref, k_ref, v_ref, kb_ref, vb_ref,
                *, d, head_dim, q_scale):
    h = _rms(x_ref[...], g_ref[...]).astype(BF16)
    grp = MXU_TILE
    ri = lax.broadcasted_iota(jnp.int32, (2 * grp, grp), 0) & (grp - 1)
    ci = lax.broadcasted_iota(jnp.int32, (2 * grp, grp), 1)
    same_head2 = jnp.where(ri // head_dim == ci // head_dim, 1.0, 0.0).astype(BF16)

    def head_norm(t, gain):
        t2 = t * t
        parts = []
        for c in range(d // grp):
            hi, lo = _split2(t2[:, c * grp:(c + 1) * grp])
            parts.append(_dot(jnp.concatenate([hi, lo], axis=-1), same_head2))
        ss = jnp.concatenate(parts, axis=-1)
        return t * lax.rsqrt(ss * (1.0 / head_dim) + EPS) * gain

    q = head_norm(_dot(h, w_ref[:, 0:d]), qg_ref[...])
    q_ref[...] = (q * q_scale).astype(BF16)
    k = head_norm(_dot(h, w_ref[:, d:2 * d]), kg_ref[...])
    k_ref[...] = k.reshape(k_ref.shape)
    kb_ref[...] = k.astype(BF16)
    v = _dot(h, w_ref[:, 2 * d:3 * d])
    v_ref[...] = v.reshape(v_ref.shape)
    vb_ref[...] = v.astype(BF16)


def _qkv_proj(x, g, w, qg, kg, heads, tm):
    m, d = x.shape
    head_dim = d // heads
    assert d % MXU_TILE == 0 and MXU_TILE % head_dim == 0
    row = lambda i: (i, 0)
    fixed = lambda i: (0, 0)
    outs = pl.pallas_call(
        functools.partial(_qkv_kernel, d=d, head_dim=head_dim, q_scale=float(head_dim) ** -0.5),
        grid=(m // tm,),
        in_specs=[pl.BlockSpec((tm, d), row),
                  pl.BlockSpec((1, d), fixed),
                  pl.BlockSpec((d, 3 * d), fixed),
                  pl.BlockSpec((1, d), fixed),
                  pl.BlockSpec((1, d), fixed)],
        out_specs=[pl.BlockSpec((tm, d), row),
                   pl.BlockSpec((tm, heads, head_dim), lambda i: (i, 0, 0)),
                   pl.BlockSpec((tm, heads, head_dim), lambda i: (i, 0, 0)),
                   pl.BlockSpec((tm, d), row),
                   pl.BlockSpec((tm, d), row)],
        out_shape=[jax.ShapeDtypeStruct((m, d), BF16),
                   jax.ShapeDtypeStruct((m, heads, head_dim), F32),
                   jax.ShapeDtypeStruct((m, heads, head_dim), F32),
                   jax.ShapeDtypeStruct((m, d), BF16),
                   jax.ShapeDtypeStruct((m, d), BF16)],
        compiler_params=_cparams(("parallel",)),
        name="qkv_proj",
    )(x, g, w, jnp.tile(qg, heads)[None, :], jnp.tile(kg, heads)[None, :])
    return outs


def _mix_ffn_kernel(x_ref, m_ref, wo_ref, g_ref, wu_ref, wd_ref, o_ref, *, ff_chunk):
    x1 = x_ref[...] + _dot(m_ref[...].astype(BF16), wo_ref[...])
    h = _rms(x1, g_ref[...]).astype(BF16)
    acc = x1
    for c in range(wu_ref.shape[1] // ff_chunk):
        u = _dot(h, wu_ref[:, c * ff_chunk:(c + 1) * ff_chunk])
        u = jnp.square(jnp.maximum(u, 0.0)).astype(BF16)
        acc = acc + _dot(u, wd_ref[c * ff_chunk:(c + 1) * ff_chunk, :])
    o_ref[...] = acc


def _mix_ffn(x, mix_in, wo, g, wu, wd, tm):
    m, d = x.shape
    dff = wu.shape[1]
    row = lambda i: (i, 0)
    fixed = lambda i: (0, 0)
    return pl.pallas_call(
        functools.partial(_mix_ffn_kernel, ff_chunk=min(dff, 1024)),
        grid=(m // tm,),
        in_specs=[pl.BlockSpec((tm, d), row),
                  pl.BlockSpec((tm, d), row),
                  pl.BlockSpec((d, d), fixed),
                  pl.BlockSpec((1, d), fixed),
                  pl.BlockSpec((d, dff), fixed),
                  pl.BlockSpec((dff, d), fixed)],
        out_specs=pl.BlockSpec((tm, d), row),
        out_shape=jax.ShapeDtypeStruct((m, d), F32),
        compiler_params=_cparams(("parallel",)),
        name="mix_ffn",
    )(x, mix_in, wo, g, wu, wd)


def _sigmoid_pair(z):
    e = jnp.exp(-jnp.abs(z))
    inv = 1.0 / (1.0 + e)
    pos = z >= 0
    return jnp.where(pos, 1.0, e) * inv, jnp.where(pos, e, 1.0) * inv


def _even_mixer_kernel(p_ref, cs_ref, hs_ref, lbl_ref, cw_ref, cb_ref, lng_ref, lnb_ref, ng_ref,
                       ab_ref, cso_ref, hso_ref, ubuf, st, pbuf, aligned,
                       *, layer, chunk, rows, dc, heads, dk, taps):
    C = chunk
    hk = heads * dk
    t_idx = pl.program_id(1)
    hist = taps - 1

    @pl.when(t_idx == 0)
    def _init():
        ubuf[...] = jnp.zeros_like(ubuf)
        ubuf[pl.ds(CONV_HALO - hist, hist), :] = cs_ref[0]
        for h in range(heads):
            st[h] = hs_ref[0, h].T

    if rows < C:
        pbuf[...] = jnp.zeros_like(pbuf)
        pbuf[pl.ds(0, rows), :] = p_ref[0]
        src = pbuf
    else:
        src = p_ref.at[0]

    val = src[:, 0:dc]
    gate = src[:, dc:2 * dc]
    u = val * _sigmoid_pair(gate)[0]
    ubuf[pl.ds(CONV_HALO, C), :] = u
    cw = cw_ref[...]
    y = jnp.zeros((C, dc), F32) + cb_ref[...]
    first = CONV_HALO - hist
    for phase in range(SUBLANES):
        offs = [first + j for j in range(taps) if (first + j) % SUBLANES == phase]
        if not offs:
            continue
        n = offs[-1] - offs[0] + C
        aligned[pl.ds(0, n), :] = ubuf[pl.ds(offs[0], n), :]
        for o in offs:
            y = y + cw[o - first:o - first + 1, :] * aligned[pl.ds(o - offs[0], C), :]
    mu = jnp.mean(y, axis=-1, keepdims=True)
    yc = y - mu
    var = jnp.mean(yc * yc, axis=-1, keepdims=True)
    y = yc * lax.rsqrt(var + EPS) * lng_ref[...] + lnb_ref[...]
    a_out = y * _sigmoid_pair(y)[0]

    @pl.when(t_idx == pl.num_programs(1) - 1)
    def _conv_state_out():
        cso_ref[0] = ubuf[pl.ds(CONV_HALO + rows - hist, hist), :]

    ubuf[pl.ds(0, CONV_HALO), :] = ubuf[pl.ds(C, CONV_HALO), :]

    lbl = lbl_ref[...]
    lmax = jnp.max(lbl, axis=0, keepdims=True)
    pe = jnp.exp(lbl - lmax)
    psm = pe / jnp.sum(pe, axis=0, keepdims=True)
    lb = jnp.zeros((1, hk), F32)
    for i in range(1, layer + 1):
        lb = lb + psm[i:i + 1, :]

    o0 = 2 * dc
    q = src[:, o0:o0 + hk]
    z = src[:, o0 + hk:o0 + 2 * hk]
    vin = src[:, o0 + 2 * hk:o0 + 3 * hk]
    og = src[:, o0 + 3 * hk:o0 + 4 * hk]
    sig, nsig = _sigmoid_pair(z)
    f = lb + (1.0 - lb) * sig
    k = (1.0 - lb) * nsig
    logf = jnp.log(f)
    row = lax.broadcasted_iota(jnp.int32, (C, hk), 0)
    if rows < C:
        valid = row < rows
        logf = jnp.where(valid, logf, 0.0)
        k = jnp.where(valid, k, 0.0)
    ri = lax.broadcasted_iota(jnp.int32, (C, C), 0)
    ci = lax.broadcasted_iota(jnp.int32, (C, C), 1)
    tril = jnp.where(ri >= ci, 1.0, 0.0).astype(BF16)
    hi, mid, lo = _split3(logf)
    b = _dot(tril, hi) + _dot(tril, mid) + _dot(tril, lo)

    def row_bcast(r, n):
        return jnp.broadcast_to(b[r:r + 1, :], (n, hk))

    b_end = b[C - 1:C, :]
    q_inter = (q * jnp.exp(b)).astype(BF16)
    k_end = (k * jnp.exp(b_end - b)).astype(BF16)
    dec = jnp.exp(b_end)
    levels = []
    m = C // 2
    while m >= HGRN_BASE:
        ref = jnp.concatenate([row_bcast(blk * 2 * m + m - 1, 2 * m) for blk in range(C // (2 * m))], axis=0)
        is_q = (row & (2 * m - 1)) >= m
        e = jnp.where(is_q, b - ref, ref - b)
        levels.append((m, (jnp.where(is_q, q, k) * jnp.exp(e)).astype(BF16)))
        m //= 2
    base = jnp.concatenate(
        [jnp.zeros((HGRN_BASE, hk), F32)]
        + [row_bcast(blk * HGRN_BASE - 1, HGRN_BASE) for blk in range(1, C // HGRN_BASE)], axis=0)
    bl = b - base
    q_diag = (q * jnp.exp(bl)).astype(BF16)
    k_diag = (k * jnp.exp(-bl)).astype(BF16)
    xor = ri ^ ci
    causal = ci <= ri
    vb = vin.astype(BF16)

    outs = []
    for h in range(heads):
        sl = slice(h * dk, (h + 1) * dk)
        att = _dot_nt(q_diag[:, sl], k_diag[:, sl])
        for m, x in reversed(levels):
            att = jnp.where(xor < m, att, _dot_nt(x[:, sl], x[:, sl]))
        att = jnp.where(causal, att, 0.0)
        s_t = st[h]
        o = _dot(att.astype(BF16), vb[:, sl]) + _dot_nt(q_inter[:, sl], s_t.astype(BF16))
        st[h] = s_t * dec[:, sl] + _dot(vin[:, sl].T.astype(BF16), k_end[:, sl])
        o = _rms(o, ng_ref[...])
        g = og[:, sl]
        outs.append(o * (g * _sigmoid_pair(g)[0]))
    b_out = jnp.concatenate(outs, axis=-1)
    ab = jnp.concatenate([a_out, b_out], axis=-1)
    ab_ref[0] = ab[0:rows, :].astype(ab_ref.dtype)

    @pl.when(t_idx == pl.num_programs(1) - 1)
    def _hgrn_state_out():
        for h in range(heads):
            hso_ref[0, h] = st[h].T


def _even_mixer(proj, conv_state, hgrn_state, lb_logits, cw, cb, lng, lnb, ng, layer):
    bsz, seq, width = proj.shape
    taps, dc = cw.shape
    heads, dk, dv = hgrn_state.shape[1:]
    assert dk == dv and width == 2 * dc + 4 * heads * dk
    chunk = HGRN_CHUNK if seq >= HGRN_CHUNK else HGRN_BASE
    rows = min(seq, chunk)
    assert seq % rows == 0
    nt = seq // rows
    kern = functools.partial(_even_mixer_kernel, layer=layer, chunk=chunk, rows=rows, dc=dc, heads=heads, dk=dk,
                             taps=taps)
    fixed2 = lambda b, t: (0, 0)
    return pl.pallas_call(
        kern,
        grid=(bsz, nt),
        in_specs=[pl.BlockSpec((1, rows, width), lambda b, t: (b, t, 0)),
                  pl.BlockSpec((1, taps - 1, dc), lambda b, t: (b, 0, 0)),
                  pl.BlockSpec((1, heads, dk, dv), lambda b, t: (b, 0, 0, 0)),
                  pl.BlockSpec(lb_logits.shape, fixed2),
                  pl.BlockSpec((taps, dc), fixed2),
                  pl.BlockSpec((1, dc), fixed2),
                  pl.BlockSpec((1, dc), fixed2),
                  pl.BlockSpec((1, dc), fixed2),
                  pl.BlockSpec((1, dv), fixed2)],
        out_specs=[pl.BlockSpec((1, rows, dc + heads * dv), lambda b, t: (b, t, 0)),
                   pl.BlockSpec((1, taps - 1, dc), lambda b, t: (b, 0, 0)),
                   pl.BlockSpec((1, heads, dk, dv), lambda b, t: (b, 0, 0, 0))],
        out_shape=[jax.ShapeDtypeStruct((bsz, seq, dc + heads * dv), BF16 if rows % 16 == 0 else F32),
                   jax.ShapeDtypeStruct((bsz, taps - 1, dc), F32),
                   jax.ShapeDtypeStruct((bsz, heads, dk, dv), F32)],
        scratch_shapes=[pltpu.VMEM((CONV_HALO + chunk + CONV_HALO, dc), F32),
                        pltpu.VMEM((heads, dv, dk), F32),
                        pltpu.VMEM((chunk, width), F32),
                        pltpu.VMEM((chunk + CONV_HALO, dc), F32)],
        compiler_params=_cparams(("parallel", "arbitrary")),
        name="even_mixer",
    )(proj, conv_state, hgrn_state, lb_logits, cw, cb[None, :], lng[None, :], lnb[None, :], ng[None, :])


def _softplus(z):
    return jnp.maximum(z, 0.0) + jnp.log(1.0 + jnp.exp(-jnp.abs(z)))


def _sb_weights(z, vis, carry, suffix2):
    sp = _softplus(z)
    if vis is not None:
        sp = jnp.where(vis, sp, 0.0)
    incl = _dot(jnp.concatenate(_split2(sp), axis=-1), suffix2)
    w = jnp.exp(z - carry - incl)
    if vis is not None:
        w = jnp.where(vis, w, 0.0)
    return w, carry + jnp.sum(sp, axis=-1, keepdims=True)


def _suffix2(n):
    j = lax.broadcasted_iota(jnp.int32, (2 * n, n), 0) & (n - 1)
    s = lax.broadcasted_iota(jnp.int32, (2 * n, n), 1)
    return jnp.where(j >= s, 1.0, 0.0).astype(BF16)


def _sb_prompt_kernel(bias_ref, q_ref, k_ref, v_ref, o_ref, vm, *, blk, hd):
    lanes = q_ref.shape[-1]
    nh = lanes // hd
    hp = pl.program_id(1)
    qi = pl.program_id(2)
    lane_head = lax.broadcasted_iota(jnp.int32, (blk, lanes), 1) // hd

    @pl.when(qi == 0)
    def _mask_values():
        def fill(r, _):
            start = pl.multiple_of(r * blk, blk)
            vf = v_ref[0, pl.ds(start, blk), :].astype(F32)
            for j in range(nh):
                vm[j, pl.ds(start, blk), :] = jnp.where(lane_head == j, vf, 0.0).astype(BF16)
            return 0
        lax.fori_loop(0, v_ref.shape[1] // blk, fill, 0)

    qf = q_ref[0].astype(F32)
    qs = [jnp.where(lane_head == j, qf, 0.0).astype(BF16) for j in range(nh)]
    biases = [bias_ref[hp * nh + j] for j in range(nh)]
    suffix2 = _suffix2(blk)
    row = lax.broadcasted_iota(jnp.int32, (blk, blk), 0)
    col = lax.broadcasted_iota(jnp.int32, (blk, blk), 1)

    def scores(kbs):
        starts = [pl.multiple_of(kb * blk, blk) for kb in kbs]
        return starts, [[_dot_nt(qs[j], k_ref[0, pl.ds(st, blk), :]) + biases[j] for j in range(nh)]
                        for st in starts]

    def soften(zs, cs, vis):
        sps = [[_softplus(z) for z in zk] for zk in zs]
        if vis is not None:
            sps[0] = [jnp.where(vis, sp, 0.0) for sp in sps[0]]
        sums = [[jnp.sum(sp, axis=-1, keepdims=True) for sp in sk] for sk in sps]
        zcs = []
        for k in range(len(zs)):
            zcs.append([zs[k][j] - cs[j] for j in range(nh)])
            cs = [cs[j] + sums[k][j] for j in range(nh)]
        if vis is not None:
            zcs[0] = [jnp.where(vis, zc, MASKED_LOGW) for zc in zcs[0]]
        return sps, zcs, cs

    def suffix_sums(sps):
        return [[_dot(jnp.concatenate(_split2(sp), axis=-1), suffix2) for sp in sk] for sk in sps]

    def weigh(starts, zcs, incl, acc):
        ws = [[jnp.exp(zcs[k][j] - incl[k][j]).astype(BF16) for j in range(nh)] for k in range(len(starts))]
        w_all = jnp.concatenate([ws[k][j] for k in range(len(starts)) for j in range(nh)], axis=1)
        v_all = jnp.concatenate([vm[j, pl.ds(st, blk), :] for st in starts for j in range(nh)], axis=0)
        return acc + _dot(w_all, v_all)

    def blocks(kbs, carry, vis):
        cs, acc = carry
        starts, zs = scores(kbs)
        sps, zcs, cs = soften(zs, cs, vis)
        return cs, weigh(starts, zcs, suffix_sums(sps), acc)

    def two_groups(kbs1, kbs2, carry):
        cs, acc = carry
        starts1, zs1 = scores(kbs1)
        sps1, zcs1, cs = soften(zs1, cs, None)
        incl1 = suffix_sums(sps1)
        starts2, zs2 = scores(kbs2)
        acc = weigh(starts1, zcs1, incl1, acc)
        sps2, zcs2, cs = soften(zs2, cs, None)
        return cs, weigh(starts2, zcs2, suffix_sums(sps2), acc)

    carry = ([jnp.zeros((blk, 1), F32) for _ in range(nh)], jnp.zeros((blk, lanes), F32))
    carry = blocks([qi], carry, col < row)
    odd = qi % 2
    carry = lax.fori_loop(0, odd, lambda i, cr: blocks([qi - 1], cr, None), carry)
    top = qi - odd
    odd_pair = (top // 2) % 2
    carry = lax.fori_loop(0, odd_pair, lambda i, cr: blocks([top - 1, top - 2], cr, None), carry)
    top = top - 2 * odd_pair
    _, acc = lax.fori_loop(
        0, top // 4,
        lambda i, cr: two_groups([top - 1 - 4 * i, top - 2 - 4 * i], [top - 3 - 4 * i, top - 4 - 4 * i], cr), carry)
    o_ref[0] = acc.astype(o_ref.dtype)


def _sb_prompt(q, k, v, bias, hd):
    bsz, seq, d = q.shape
    lanes = ATT_LANES
    blk = min(ATT_BLK, seq)
    assert seq % blk == 0 and d % lanes == 0 and lanes % hd == 0 and blk & (blk - 1) == 0
    return pl.pallas_call(
        functools.partial(_sb_prompt_kernel, blk=blk, hd=hd),
        grid=(bsz, d // lanes, seq // blk),
        in_specs=[pl.BlockSpec(memory_space=pltpu.SMEM),
                  pl.BlockSpec((1, blk, lanes), lambda b, h, i: (b, i, h)),
                  pl.BlockSpec((1, seq, lanes), lambda b, h, i: (b, 0, h)),
                  pl.BlockSpec((1, seq, lanes), lambda b, h, i: (b, 0, h))],
        out_specs=pl.BlockSpec((1, blk, lanes), lambda b, h, i: (b, i, h)),
        out_shape=jax.ShapeDtypeStruct((bsz, seq, d), BF16),
        scratch_shapes=[pltpu.VMEM((lanes // hd, seq, lanes), BF16)],
        compiler_params=_cparams(("parallel", "parallel", "arbitrary")),
        name="sb_prompt",
    )(bias, q, k, v)


def _sb_decode_kernel(pt_ref, bias_ref, q_ref, kn_ref, vn_ref, *rest, heads, pages_per_step, psize, nq):
    kp = rest[:pages_per_step]
    vp = rest[pages_per_step:2 * pages_per_step]
    o_ref = rest[2 * pages_per_step]
    qbd, acc, cbuf, kpad, vpad = rest[2 * pages_per_step + 1:]
    del pt_ref
    step = pl.program_id(1)
    d = q_ref.shape[-1]
    hd = d // heads
    rows = nq * heads
    lane_head = lax.broadcasted_iota(jnp.int32, (rows, d), 1) // hd
    row_head = lax.broadcasted_iota(jnp.int32, (rows, d), 0) & (heads - 1)
    own = lane_head == row_head
    suffix2 = _suffix2(psize)
    bias = bias_ref[...]

    def attend_new_rows(kblk, vblk, vis):
        z = _dot_nt(qbd[...], kblk) + bias
        w, c = _sb_weights(z, vis, cbuf[...], suffix2)
        acc[...] += _dot(w.astype(BF16), vblk)
        cbuf[...] = c

    @pl.when(step == 0)
    def _new_rows():
        qv = q_ref[0]
        qbd[...] = jnp.where(own, jnp.concatenate(
            [jnp.broadcast_to(qv[t:t + 1, :], (heads, d)) for t in range(nq)], axis=0), 0.0).astype(BF16)
        acc[...] = jnp.zeros_like(acc)
        cbuf[...] = jnp.zeros_like(cbuf)
        kpad[...] = jnp.zeros_like(kpad)
        vpad[...] = jnp.zeros_like(vpad)
        kpad[pl.ds(0, nq), :] = kn_ref[0]
        vpad[pl.ds(0, nq), :] = vn_ref[0]
        s_idx = lax.broadcasted_iota(jnp.int32, (rows, psize), 1)
        t_idx = lax.broadcasted_iota(jnp.int32, (rows, psize), 0) // heads
        attend_new_rows(kpad[...].astype(BF16), vpad[...].astype(BF16), s_idx < t_idx)

    q_bd = qbd[...]
    zs = [_dot(q_bd, kp[i][...].astype(BF16)) + bias for i in range(pages_per_step)]
    sps = [_softplus(z) for z in zs]
    c = cbuf[...]
    zcs = []
    for i in range(pages_per_step):
        zcs.append(zs[i] - c)
        c = c + jnp.sum(sps[i], axis=-1, keepdims=True)
    cbuf[...] = c
    incl = [_dot(jnp.concatenate(_split2(sp), axis=-1), suffix2) for sp in sps]
    ws = [jnp.exp(zcs[i] - incl[i]).astype(BF16) for i in range(pages_per_step)]
    out = _dot_nt(ws[0], vp[0][...].astype(BF16))
    for i in range(1, pages_per_step):
        out = out + _dot_nt(ws[i], vp[i][...].astype(BF16))
    acc[...] += out

    @pl.when(step == pl.num_programs(1) - 1)
    def _finish():
        a = jnp.where(own, acc[...], 0.0)
        o_ref[0] = jnp.concatenate(
            [jnp.sum(a[t * heads:(t + 1) * heads, :], axis=0, keepdims=True) for t in range(nq)],
            axis=0).astype(o_ref.dtype)


def _sb_decode(q, k_new, v_new, cache_k, cache_v, page_table, bias, layer, heads):
    bsz, nq, d = q.shape
    n_pages = page_table.shape[1]
    psize = cache_k.shape[3]
    pps = DECODE_PAGES_PER_STEP if n_pages % DECODE_PAGES_PER_STEP == 0 else 1
    steps = n_pages // pps
    rows = nq * heads
    assert heads & (heads - 1) == 0 and psize & (psize - 1) == 0 and cache_k.shape[2] == d

    def page_spec(i):
        def imap(b, s, pt):
            return (layer, pt[b, n_pages - 1 - (s * pps + i)], 0, 0)
        return pl.BlockSpec((None, None, d, psize), imap)

    tok = lambda b, s, pt: (b, 0, 0)
    bias_rows = jnp.tile(bias, nq)[:, None].astype(F32)
    grid_spec = pltpu.PrefetchScalarGridSpec(
        num_scalar_prefetch=1,
        grid=(bsz, steps),
        in_specs=[pl.BlockSpec((rows, 1), lambda b, s, pt: (0, 0)),
                  pl.BlockSpec((1, nq, d), tok),
                  pl.BlockSpec((1, nq, d), tok),
                  pl.BlockSpec((1, nq, d), tok)]
                 + [page_spec(i) for i in range(pps)] + [page_spec(i) for i in range(pps)],
        out_specs=pl.BlockSpec((1, nq, d), tok),
        scratch_shapes=[pltpu.VMEM((rows, d), BF16),
                        pltpu.VMEM((rows, d), F32),
                        pltpu.VMEM((rows, 1), F32),
                        pltpu.VMEM((psize, d), F32),
                        pltpu.VMEM((psize, d), F32)],
    )
    return pl.pallas_call(
        functools.partial(_sb_decode_kernel, heads=heads, pages_per_step=pps, psize=psize, nq=nq),
        grid_spec=grid_spec,
        out_shape=jax.ShapeDtypeStruct((bsz, nq, d), F32),
        compiler_params=_cparams(("parallel", "arbitrary")),
        name="sb_decode",
    )(page_table, bias_rows, q, k_new, v_new, *([cache_k] * pps), *([cache_v] * pps))


def _trunk(x, conv_state, hgrn_state, cache_k, cache_v, page_table, w, tm):
    bsz, seq, d = x.shape
    depth = w['norm_mix'].shape[0]
    heads = w['sb_bias'].shape[1]
    hd = d // heads
    xf = x.reshape(bsz * seq, d)
    new_conv, new_hgrn, new_k, new_v = [], [], [], []
    for l in range(depth):
        if l % 2 == 0:
            e = l // 2
            proj = _norm_matmul(xf, w['norm_mix'][l][None, :], w['w_in_even'][e], tm)
            ab, cso, hso = _even_mixer(proj.reshape(bsz, seq, -1), conv_state[e], hgrn_state[e],
                                       w['hgrn_lb_logits'], w['conv_w'][e], w['conv_b'][e],
                                       w['conv_ln_g'][e], w['conv_ln_b'][e], w['hgrn_norm_g'][e], e)
            new_conv.append(cso)
            new_hgrn.append(hso)
            mix_in, w_mix = ab.reshape(bsz * seq, d), w['w_out_even'][e]
        else:
            a = l // 2
            q, k, v, kb, vb = _qkv_proj(xf, w['norm_mix'][l][None, :], w['w_qkv'][a],
                                        w['q_norm_g'][a], w['k_norm_g'][a], heads, tm)
            if cache_k is None:
                o = _sb_prompt(q.reshape(bsz, seq, d), kb.reshape(bsz, seq, d), vb.reshape(bsz, seq, d),
                               w['sb_bias'][a], hd)
            else:
                as_rows = lambda t: t.astype(F32).reshape(bsz, seq, d)
                o = _sb_decode(as_rows(q), as_rows(kb), as_rows(vb),
                               cache_k, cache_v, page_table, w['sb_bias'][a], a, heads)
            new_k.append(k.reshape(bsz, seq, heads, hd))
            new_v.append(v.reshape(bsz, seq, heads, hd))
            mix_in, w_mix = o.reshape(bsz * seq, d), w['w_o'][a]
        xf = _mix_ffn(xf, mix_in, w_mix, w['norm_ffn'][l][None, :], w['w_up'][l], w['w_down'][l], tm)
    return (xf.reshape(bsz, seq, d), jnp.stack(new_conv), jnp.stack(new_hgrn),
            jnp.stack(new_k), jnp.stack(new_v))


def kernel(x_prompt, x_sample, state_conv, state_hgrn, cache_k, cache_v, page_table, norm_mix, norm_ffn, w_in_even, conv_w, conv_b, conv_ln_g, conv_ln_b, hgrn_lb_logits, hgrn_norm_g, w_out_even, w_qkv, q_norm_g, k_norm_g, sb_bias, w_o, w_up, w_down):
    w = dict(norm_mix=norm_mix, norm_ffn=norm_ffn, conv_w=conv_w, conv_b=conv_b, conv_ln_g=conv_ln_g,
             conv_ln_b=conv_ln_b, hgrn_lb_logits=hgrn_lb_logits, hgrn_norm_g=hgrn_norm_g,
             q_norm_g=q_norm_g, k_norm_g=k_norm_g, sb_bias=sb_bias)
    for name, t in dict(w_in_even=w_in_even, w_out_even=w_out_even, w_qkv=w_qkv, w_o=w_o,
                        w_up=w_up, w_down=w_down).items():
        w[name] = t.astype(BF16)
    bsz, seq, d = x_prompt.shape
    n_mix = state_conv.shape[0]
    conv0 = jnp.zeros((n_mix, bsz) + state_conv.shape[2:], F32)
    hgrn0 = jnp.zeros((n_mix, bsz) + state_hgrn.shape[2:], F32)
    tm_p = 256 if (bsz * seq) % 256 == 0 else bsz * seq
    y_p, conv_p, hgrn_p, k_p, v_p = _trunk(x_prompt, conv0, hgrn0, None, None, None, w, tm_p)
    sb, ss, _ = x_sample.shape
    lay, pages, psize = cache_k.shape[:3]
    ck = jnp.transpose(cache_k, (0, 1, 3, 4, 2)).reshape(lay, pages, d, psize)
    cv = jnp.transpose(cache_v, (0, 1, 3, 4, 2)).reshape(lay, pages, d, psize)
    y_s, conv_s, hgrn_s, k_s, v_s = _trunk(x_sample, state_conv, state_hgrn, ck, cv, page_table, w, sb * ss)
    return (y_p, y_s, conv_p, conv_s, hgrn_p, hgrn_s, k_p, v_p, k_s, v_s)
```

```python
import functools

import jax
import jax.numpy as jnp
from jax import lax
from jax.experimental import pallas as pl
from jax.experimental.pallas import tpu as pltpu

F32 = jnp.float32
BF16 = jnp.bfloat16

EPS = 1e-6
HGRN_CHUNK = 128
HGRN_BASE = 16
CONV_HALO = 32
ATT_BLK = 256
ATT_LANES = 256
ATT_GROUP = 2
DECODE_PAGES_PER_STEP = 16
VMEM_LIMIT = 56 * 1024 * 1024
MXU_TILE = 256
SUBLANES = 8
MASKED_LOGW = -1e30

def _cparams(sem, flags=None):
    return pltpu.CompilerParams(dimension_semantics=sem, vmem_limit_bytes=VMEM_LIMIT, flags=flags)


def _rms(x, g):
    ms = jnp.mean(x * x, axis=-1, keepdims=True)
    return x * lax.rsqrt(ms + EPS) * g


def _dot(a, b):
    return jnp.dot(a, b, preferred_element_type=F32)


def _dot_nt(a, b):
    return lax.dot_general(a, b, (((1,), (1,)), ((), ())), preferred_element_type=F32)


def _split2(x):
    hi = x.astype(BF16)
    lo = (x - hi.astype(F32)).astype(BF16)
    return hi, lo


def _split3(x):
    hi = x.astype(BF16)
    r = x - hi.astype(F32)
    mid = r.astype(BF16)
    lo = (r - mid.astype(F32)).astype(BF16)
    return hi, mid, lo


def _norm_matmul_kernel(x_ref, g_ref, w_ref, o_ref):
    h = _rms(x_ref[...], g_ref[...]).astype(BF16)
    o_ref[...] = _dot(h, w_ref[...])


def _norm_matmul(x, g, w, tm):
    m, d = x.shape
    n = w.shape[1]
    return pl.pallas_call(
        _norm_matmul_kernel,
        grid=(m // tm,),
        in_specs=[pl.BlockSpec((tm, d), lambda i: (i, 0)),
                  pl.BlockSpec((1, d), lambda i: (0, 0)),
                  pl.BlockSpec((d, n), lambda i: (0, 0))],
        out_specs=pl.BlockSpec((tm, n), lambda i: (i, 0)),
        out_shape=jax.ShapeDtypeStruct((m, n), F32),
        compiler_params=_cparams(("parallel",)),
        name="norm_matmul",
    )(x, g, w)


def _qkv_kernel(x_ref, g_ref, w_ref, qg_ref, kg_ref, q_ref, k_ref, v_ref, kb_ref, vb_ref,
                *, d, head_dim, q_scale):
    h = _rms(x_ref[...], g_ref[...]).astype(BF16)
    grp = MXU_TILE
    ri = lax.broadcasted_iota(jnp.int32, (2 * grp, grp), 0) & (grp - 1)
    ci = lax.broadcasted_iota(jnp.int32, (2 * grp, grp), 1)
    same_head2 = jnp.where(ri // head_dim == ci // head_dim, 1.0, 0.0).astype(BF16)

    def head_norm(t, gain):
        t2 = t * t
        parts = []
        for c in range(d // grp):
            hi, lo = _split2(t2[:, c * grp:(c + 1) * grp])
            parts.append(_dot(jnp.concatenate([hi, lo], axis=-1), same_head2))
        ss = jnp.concatenate(parts, axis=-1)
        return t * lax.rsqrt(ss * (1.0 / head_dim) + EPS) * gain

    q = head_norm(_dot(h, w_ref[:, 0:d]), qg_ref[...])
    q_ref[...] = (q * q_scale).astype(BF16)
    k = head_norm(_dot(h, w_ref[:, d:2 * d]), kg_ref[...])
    k_ref[...] = k.reshape(k_ref.shape)
    kb_ref[...] = k.astype(BF16)
    v = _dot(h, w_ref[:, 2 * d:3 * d])
    v_ref[...] = v.reshape(v_ref.shape)
    vb_ref[...] = v.astype(BF16)


def _qkv_proj(x, g, w, qg, kg, heads, tm):
    m, d = x.shape
    head_dim = d // heads
    assert d % MXU_TILE == 0 and MXU_TILE % head_dim == 0
    row = lambda i: (i, 0)
    fixed = lambda i: (0, 0)
    outs = pl.pallas_call(
        functools.partial(_qkv_kernel, d=d, head_dim=head_dim, q_scale=float(head_dim) ** -0.5),
        grid=(m // tm,),
        in_specs=[pl.BlockSpec((tm, d), row),
                  pl.BlockSpec((1, d), fixed),
                  pl.BlockSpec((d, 3 * d), fixed),
                  pl.BlockSpec((1, d), fixed),
                  pl.BlockSpec((1, d), fixed)],
        out_specs=[pl.BlockSpec((tm, d), row),
                   pl.BlockSpec((tm, heads, head_dim), lambda i: (i, 0, 0)),
                   pl.BlockSpec((tm, heads, head_dim), lambda i: (i, 0, 0)),
                   pl.BlockSpec((tm, d), row),
                   pl.BlockSpec((tm, d), row)],
        out_shape=[jax.ShapeDtypeStruct((m, d), BF16),
                   jax.ShapeDtypeStruct((m, heads, head_dim), F32),
                   jax.ShapeDtypeStruct((m, heads, head_dim), F32),
                   jax.ShapeDtypeStruct((m, d), BF16),
                   jax.ShapeDtypeStruct((m, d), BF16)],
        compiler_params=_cparams(("parallel",)),
        name="qkv_proj",
    )(x, g, w, jnp.tile(qg, heads)[None, :], jnp.tile(kg, heads)[None, :])
    return outs


def _mix_ffn_kernel(x_ref, m_ref, wo_ref, g_ref, wu_ref, wd_ref, o_ref, *, ff_chunk):
    x1 = x_ref[...] + _dot(m_ref[...].astype(BF16), wo_ref[...])
    h = _rms(x1, g_ref[...]).astype(BF16)
    acc = x1
    for c in range(wu_ref.shape[1] // ff_chunk):
        u = _dot(h, wu_ref[:, c * ff_chunk:(c + 1) * ff_chunk])
        u = jnp.square(jnp.maximum(u, 0.0)).astype(BF16)
        acc = acc + _dot(u, wd_ref[c * ff_chunk:(c + 1) * ff_chunk, :])
    o_ref[...] = acc


def _mix_ffn(x, mix_in, wo, g, wu, wd, tm):
    m, d = x.shape
    dff = wu.shape[1]
    row = lambda i: (i, 0)
    fixed = lambda i: (0, 0)
    return pl.pallas_call(
        functools.partial(_mix_ffn_kernel, ff_chunk=min(dff, 1024)),
        grid=(m // tm,),
        in_specs=[pl.BlockSpec((tm, d), row),
                  pl.BlockSpec((tm, d), row),
                  pl.BlockSpec((d, d), fixed),
                  pl.BlockSpec((1, d), fixed),
                  pl.BlockSpec((d, dff), fixed),
                  pl.BlockSpec((dff, d), fixed)],
        out_specs=pl.BlockSpec((tm, d), row),
        out_shape=jax.ShapeDtypeStruct((m, d), F32),
        compiler_params=_cparams(("parallel",)),
        name="mix_ffn",
    )(x, mix_in, wo, g, wu, wd)


def _sigmoid_pair(z):
    e = jnp.exp(-jnp.abs(z))
    inv = 1.0 / (1.0 + e)
    pos = z >= 0
    return jnp.where(pos, 1.0, e) * inv, jnp.where(pos, e, 1.0) * inv


def _even_mixer_kernel(p_ref, cs_ref, hs_ref, lbl_ref, cw_ref, cb_ref, lng_ref, lnb_ref, ng_ref,
                       ab_ref, cso_ref, hso_ref, ubuf, st, pbuf, aligned,
                       *, layer, chunk, rows, dc, heads, dk, taps):
    C = chunk
    hk = heads * dk
    t_idx = pl.program_id(1)
    hist = taps - 1

    @pl.when(t_idx == 0)
    def _init():
        ubuf[...] = jnp.zeros_like(ubuf)
        ubuf[pl.ds(CONV_HALO - hist, hist), :] = cs_ref[0]
        for h in range(heads):
            st[h] = hs_ref[0, h].T

    if rows < C:
        pbuf[...] = jnp.zeros_like(pbuf)
        pbuf[pl.ds(0, rows), :] = p_ref[0]
        src = pbuf
    else:
        src = p_ref.at[0]

    val = src[:, 0:dc]
    gate = src[:, dc:2 * dc]
    u = val * _sigmoid_pair(gate)[0]
    ubuf[pl.ds(CONV_HALO, C), :] = u
    cw = cw_ref[...]
    y = jnp.zeros((C, dc), F32) + cb_ref[...]
    first = CONV_HALO - hist
    for phase in range(SUBLANES):
        offs = [first + j for j in range(taps) if (first + j) % SUBLANES == phase]
        if not offs:
            continue
        n = offs[-1] - offs[0] + C
        aligned[pl.ds(0, n), :] = ubuf[pl.ds(offs[0], n), :]
        for o in offs:
            y = y + cw[o - first:o - first + 1, :] * aligned[pl.ds(o - offs[0], C), :]
    mu = jnp.mean(y, axis=-1, keepdims=True)
    yc = y - mu
    var = jnp.mean(yc * yc, axis=-1, keepdims=True)
    y = yc * lax.rsqrt(var + EPS) * lng_ref[...] + lnb_ref[...]
    a_out = y * _sigmoid_pair(y)[0]

    @pl.when(t_idx == pl.num_programs(1) - 1)
    def _conv_state_out():
        cso_ref[0] = ubuf[pl.ds(CONV_HALO + rows - hist, hist), :]

    ubuf[pl.ds(0, CONV_HALO), :] = ubuf[pl.ds(C, CONV_HALO), :]

    lbl = lbl_ref[...]
    lmax = jnp.max(lbl, axis=0, keepdims=True)
    pe = jnp.exp(lbl - lmax)
    psm = pe / jnp.sum(pe, axis=0, keepdims=True)
    lb = jnp.zeros((1, hk), F32)
    for i in range(1, layer + 1):
        lb = lb + psm[i:i + 1, :]

    o0 = 2 * dc
    q = src[:, o0:o0 + hk]
    z = src[:, o0 + hk:o0 + 2 * hk]
    vin = src[:, o0 + 2 * hk:o0 + 3 * hk]
    og = src[:, o0 + 3 * hk:o0 + 4 * hk]
    sig, nsig = _sigmoid_pair(z)
    f = lb + (1.0 - lb) * sig
    k = (1.0 - lb) * nsig
    logf = jnp.log(f)
    row = lax.broadcasted_iota(jnp.int32, (C, hk), 0)
    if rows < C:
        valid = row < rows
        logf = jnp.where(valid, logf, 0.0)
        k = jnp.where(valid, k, 0.0)
    ri = lax.broadcasted_iota(jnp.int32, (C, C), 0)
    ci = lax.broadcasted_iota(jnp.int32, (C, C), 1)
    tril = jnp.where(ri >= ci, 1.0, 0.0).astype(BF16)
    hi, mid, lo = _split3(logf)
    b = _dot(tril, hi) + _dot(tril, mid) + _dot(tril, lo)

    def row_bcast(r, n):
        return jnp.broadcast_to(b[r:r + 1, :], (n, hk))

    b_end = b[C - 1:C, :]
    q_inter = (q * jnp.exp(b)).astype(BF16)
    k_end = (k * jnp.exp(b_end - b)).astype(BF16)
    dec = jnp.exp(b_end)
    levels = []
    m = C // 2
    while m >= HGRN_BASE:
        ref = jnp.concatenate([row_bcast(blk * 2 * m + m - 1, 2 * m) for blk in range(C // (2 * m))], axis=0)
        is_q = (row & (2 * m - 1)) >= m
        e = jnp.where(is_q, b - ref, ref - b)
        levels.append((m, (jnp.where(is_q, q, k) * jnp.exp(e)).astype(BF16)))
        m //= 2
    base = jnp.concatenate(
        [jnp.zeros((HGRN_BASE, hk), F32)]
        + [row_bcast(blk * HGRN_BASE - 1, HGRN_BASE) for blk in range(1, C // HGRN_BASE)], axis=0)
    bl = b - base
    q_diag = (q * jnp.exp(bl)).astype(BF16)
    k_diag = (k * jnp.exp(-bl)).astype(BF16)
    xor = ri ^ ci
    causal = ci <= ri
    vb = vin.astype(BF16)

    outs = []
    for h in range(heads):
        sl = slice(h * dk, (h + 1) * dk)
        att = _dot_nt(q_diag[:, sl], k_diag[:, sl])
        for m, x in reversed(levels):
            att = jnp.where(xor < m, att, _dot_nt(x[:, sl], x[:, sl]))
        att = jnp.where(causal, att, 0.0)
        s_t = st[h]
        o = _dot(att.astype(BF16), vb[:, sl]) + _dot_nt(q_inter[:, sl], s_t.astype(BF16))
        st[h] = s_t * dec[:, sl] + _dot(vin[:, sl].T.astype(BF16), k_end[:, sl])
        o = _rms(o, ng_ref[...])
        g = og[:, sl]
        outs.append(o * (g * _sigmoid_pair(g)[0]))
    b_out = jnp.concatenate(outs, axis=-1)
    ab = jnp.concatenate([a_out, b_out], axis=-1)
    ab_ref[0] = ab[0:rows, :].astype(ab_ref.dtype)

    @pl.when(t_idx == pl.num_programs(1) - 1)
    def _hgrn_state_out():
        for h in range(heads):
            hso_ref[0, h] = st[h].T


def _even_mixer(proj, conv_state, hgrn_state, lb_logits, cw, cb, lng, lnb, ng, layer):
    bsz, seq, width = proj.shape
    taps, dc = cw.shape
    heads, dk, dv = hgrn_state.shape[1:]
    assert dk == dv and width == 2 * dc + 4 * heads * dk
    chunk = HGRN_CHUNK if seq >= HGRN_CHUNK else HGRN_BASE
    rows = min(seq, chunk)
    assert seq % rows == 0
    nt = seq // rows
    kern = functools.partial(_even_mixer_kernel, layer=layer, chunk=chunk, rows=rows, dc=dc, heads=heads, dk=dk,
                             taps=taps)
    fixed2 = lambda b, t: (0, 0)
    return pl.pallas_call(
        kern,
        grid=(bsz, nt),
        in_specs=[pl.BlockSpec((1, rows, width), lambda b, t: (b, t, 0)),
                  pl.BlockSpec((1, taps - 1, dc), lambda b, t: (b, 0, 0)),
                  pl.BlockSpec((1, heads, dk, dv), lambda b, t: (b, 0, 0, 0)),
                  pl.BlockSpec(lb_logits.shape, fixed2),
                  pl.BlockSpec((taps, dc), fixed2),
                  pl.BlockSpec((1, dc), fixed2),
                  pl.BlockSpec((1, dc), fixed2),
                  pl.BlockSpec((1, dc), fixed2),
                  pl.BlockSpec((1, dv), fixed2)],
        out_specs=[pl.BlockSpec((1, rows, dc + heads * dv), lambda b, t: (b, t, 0)),
                   pl.BlockSpec((1, taps - 1, dc), lambda b, t: (b, 0, 0)),
                   pl.BlockSpec((1, heads, dk, dv), lambda b, t: (b, 0, 0, 0))],
        out_shape=[jax.ShapeDtypeStruct((bsz, seq, dc + heads * dv), BF16 if rows % 16 == 0 else F32),
                   jax.ShapeDtypeStruct((bsz, taps - 1, dc), F32),
                   jax.ShapeDtypeStruct((bsz, heads, dk, dv), F32)],
        scratch_shapes=[pltpu.VMEM((CONV_HALO + chunk + CONV_HALO, dc), F32),
                        pltpu.VMEM((heads, dv, dk), F32),
                        pltpu.VMEM((chunk, width), F32),
                        pltpu.VMEM((chunk + CONV_HALO, dc), F32)],
        compiler_params=_cparams(("parallel", "arbitrary")),
        name="even_mixer",
    )(proj, conv_state, hgrn_state, lb_logits, cw, cb[None, :], lng[None, :], lnb[None, :], ng[None, :])


def _softplus(z):
    return jnp.maximum(z, 0.0) + jnp.log(1.0 + jnp.exp(-jnp.abs(z)))


def _sb_weights(z, vis, carry, suffix2):
    sp = _softplus(z)
    if vis is not None:
        sp = jnp.where(vis, sp, 0.0)
    incl = _dot(jnp.concatenate(_split2(sp), axis=-1), suffix2)
    w = jnp.exp(z - carry - incl)
    if vis is not None:
        w = jnp.where(vis, w, 0.0)
    return w, carry + jnp.sum(sp, axis=-1, keepdims=True)


def _suffix2(n):
    j = lax.broadcasted_iota(jnp.int32, (2 * n, n), 0) & (n - 1)
    s = lax.broadcasted_iota(jnp.int32, (2 * n, n), 1)
    return jnp.where(j >= s, 1.0, 0.0).astype(BF16)


def _sb_prompt_kernel(bias_ref, q_ref, k_ref, v_ref, o_ref, vm, *, blk, hd):
    lanes = q_ref.shape[-1]
    nh = lanes // hd
    hp = pl.program_id(1)
    qi = pl.program_id(2)
    lane_head = lax.broadcasted_iota(jnp.int32, (blk, lanes), 1) // hd

    @pl.when(qi == 0)
    def _mask_values():
        def fill(r, _):
            start = pl.multiple_of(r * blk, blk)
            vf = v_ref[0, pl.ds(start, blk), :].astype(F32)
            for j in range(nh):
                vm[j, pl.ds(start, blk), :] = jnp.where(lane_head == j, vf, 0.0).astype(BF16)
            return 0
        lax.fori_loop(0, v_ref.shape[1] // blk, fill, 0)

    qf = q_ref[0].astype(F32)
    qs = [jnp.where(lane_head == j, qf, 0.0).astype(BF16) for j in range(nh)]
    biases = [bias_ref[hp * nh + j] for j in range(nh)]
    suffix2 = _suffix2(blk)
    row = lax.broadcasted_iota(jnp.int32, (blk, blk), 0)
    col = lax.broadcasted_iota(jnp.int32, (blk, blk), 1)

    def scores(kbs):
        starts = [pl.multiple_of(kb * blk, blk) for kb in kbs]
        return starts, [[_dot_nt(qs[j], k_ref[0, pl.ds(st, blk), :]) + biases[j] for j in range(nh)]
                        for st in starts]

    def soften(zs, cs, vis):
        sps = [[_softplus(z) for z in zk] for zk in zs]
        if vis is not None:
            sps[0] = [jnp.where(vis, sp, 0.0) for sp in sps[0]]
        sums = [[jnp.sum(sp, axis=-1, keepdims=True) for sp in sk] for sk in sps]
        zcs = []
        for k in range(len(zs)):
            zcs.append([zs[k][j] - cs[j] for j in range(nh)])
            cs = [cs[j] + sums[k][j] for j in range(nh)]
        if vis is not None:
            zcs[0] = [jnp.where(vis, zc, MASKED_LOGW) for zc in zcs[0]]
        return sps, zcs, cs

    def suffix_sums(sps):
        return [[_dot(jnp.concatenate(_split2(sp), axis=-1), suffix2) for sp in sk] for sk in sps]

    def weigh(starts, zcs, incl, acc):
        ws = [[jnp.exp(zcs[k][j] - incl[k][j]).astype(BF16) for j in range(nh)] for k in range(len(starts))]
        w_all = jnp.concatenate([ws[k][j] for k in range(len(starts)) for j in range(nh)], axis=1)
        v_all = jnp.concatenate([vm[j, pl.ds(st, blk), :] for st in starts for j in range(nh)], axis=0)
        return acc + _dot(w_all, v_all)

    def blocks(kbs, carry, vis):
        cs, acc = carry
        starts, zs = scores(kbs)
        sps, zcs, cs = soften(zs, cs, vis)
        return cs, weigh(starts, zcs, suffix_sums(sps), acc)

    def two_groups(kbs1, kbs2, carry):
        cs, acc = carry
        starts1, zs1 = scores(kbs1)
        sps1, zcs1, cs = soften(zs1, cs, None)
        incl1 = suffix_sums(sps1)
        starts2, zs2 = scores(kbs2)
        acc = weigh(starts1, zcs1, incl1, acc)
        sps2, zcs2, cs = soften(zs2, cs, None)
        return cs, weigh(starts2, zcs2, suffix_sums(sps2), acc)

    carry = ([jnp.zeros((blk, 1), F32) for _ in range(nh)], jnp.zeros((blk, lanes), F32))
    carry = blocks([qi], carry, col < row)
    odd = qi % 2
    carry = lax.fori_loop(0, odd, lambda i, cr: blocks([qi - 1], cr, None), carry)
    top = qi - odd
    odd_pair = (top // 2) % 2
    carry = lax.fori_loop(0, odd_pair, lambda i, cr: blocks([top - 1, top - 2], cr, None), carry)
    top = top - 2 * odd_pair
    _, acc = lax.fori_loop(
        0, top // 4,
        lambda i, cr: two_groups([top - 1 - 4 * i, top - 2 - 4 * i], [top - 3 - 4 * i, top - 4 - 4 * i], cr), carry)
    o_ref[0] = acc.astype(o_ref.dtype)


def _sb_prompt(q, k, v, bias, hd):
    bsz, seq, d = q.shape
    lanes = ATT_LANES
    blk = min(ATT_BLK, seq)
    assert seq % blk == 0 and d % lanes == 0 and lanes % hd == 0 and blk & (blk - 1) == 0
    return pl.pallas_call(
        functools.partial(_sb_prompt_kernel, blk=blk, hd=hd),
        grid=(bsz, d // lanes, seq // blk),
        in_specs=[pl.BlockSpec(memory_space=pltpu.SMEM),
                  pl.BlockSpec((1, blk, lanes), lambda b, h, i: (b, i, h)),
                  pl.BlockSpec((1, seq, lanes), lambda b, h, i: (b, 0, h)),
                  pl.BlockSpec((1, seq, lanes), lambda b, h, i: (b, 0, h))],
        out_specs=pl.BlockSpec((1, blk, lanes), lambda b, h, i: (b, i, h)),
        out_shape=jax.ShapeDtypeStruct((bsz, seq, d), BF16),
        scratch_shapes=[pltpu.VMEM((lanes // hd, seq, lanes), BF16)],
        compiler_params=_cparams(("parallel", "parallel", "arbitrary")),
        name="sb_prompt",
    )(bias, q, k, v)


def _sb_decode_kernel(pt_ref, bias_ref, q_ref, kn_ref, vn_ref, *rest, heads, pages_per_step, psize, nq):
    kp = rest[:pages_per_step]
    vp = rest[pages_per_step:2 * pages_per_step]
    o_ref = rest[2 * pages_per_step]
    qbd, acc, cbuf, kpad, vpad = rest[2 * pages_per_step + 1:]
    del pt_ref
    step = pl.program_id(1)
    d = q_ref.shape[-1]
    hd = d // heads
    rows = nq * heads
    lane_head = lax.broadcasted_iota(jnp.int32, (rows, d), 1) // hd
    row_head = lax.broadcasted_iota(jnp.int32, (rows, d), 0) & (heads - 1)
    own = lane_head == row_head
    suffix2 = _suffix2(psize)
    bias = bias_ref[...]

    def attend_new_rows(kblk, vblk, vis):
        z = _dot_nt(qbd[...], kblk) + bias
        w, c = _sb_weights(z, vis, cbuf[...], suffix2)
        acc[...] += _dot(w.astype(BF16), vblk)
        cbuf[...] = c

    @pl.when(step == 0)
    def _new_rows():
        qv = q_ref[0]
        qbd[...] = jnp.where(own, jnp.concatenate(
            [jnp.broadcast_to(qv[t:t + 1, :], (heads, d)) for t in range(nq)], axis=0), 0.0).astype(BF16)
        acc[...] = jnp.zeros_like(acc)
        cbuf[...] = jnp.zeros_like(cbuf)
        kpad[...] = jnp.zeros_like(kpad)
        vpad[...] = jnp.zeros_like(vpad)
        kpad[pl.ds(0, nq), :] = kn_ref[0]
        vpad[pl.ds(0, nq), :] = vn_ref[0]
        s_idx = lax.broadcasted_iota(jnp.int32, (rows, psize), 1)
        t_idx = lax.broadcasted_iota(jnp.int32, (rows, psize), 0) // heads
        attend_new_rows(kpad[...].astype(BF16), vpad[...].astype(BF16), s_idx < t_idx)

    q_bd = qbd[...]
    zs = [_dot(q_bd, kp[i][...].astype(BF16)) + bias for i in range(pages_per_step)]
    sps = [_softplus(z) for z in zs]
    c = cbuf[...]
    zcs = []
    for i in range(pages_per_step):
        zcs.append(zs[i] - c)
        c = c + jnp.sum(sps[i], axis=-1, keepdims=True)
    cbuf[...] = c
    incl = [_dot(jnp.concatenate(_split2(sp), axis=-1), suffix2) for sp in sps]
    ws = [jnp.exp(zcs[i] - incl[i]).astype(BF16) for i in range(pages_per_step)]
    out = _dot_nt(ws[0], vp[0][...].astype(BF16))
    for i in range(1, pages_per_step):
        out = out + _dot_nt(ws[i], vp[i][...].astype(BF16))
    acc[...] += out

    @pl.when(step == pl.num_programs(1) - 1)
    def _finish():
        a = jnp.where(own, acc[...], 0.0)
        o_ref[0] = jnp.concatenate(
            [jnp.sum(a[t * heads:(t + 1) * heads, :], axis=0, keepdims=True) for t in range(nq)],
            axis=0).astype(o_ref.dtype)


def _sb_decode(q, k_new, v_new, cache_k, cache_v, page_table, bias, layer, heads):
    bsz, nq, d = q.shape
    n_pages = page_table.shape[1]
    psize = cache_k.shape[3]
    pps = DECODE_PAGES_PER_STEP if n_pages % DECODE_PAGES_PER_STEP == 0 else 1
    steps = n_pages // pps
    rows = nq * heads
    assert heads & (heads - 1) == 0 and psize & (psize - 1) == 0 and cache_k.shape[2] == d

    def page_spec(i):
        def imap(b, s, pt):
            return (layer, pt[b, n_pages - 1 - (s * pps + i)], 0, 0)
        return pl.BlockSpec((None, None, d, psize), imap)

    tok = lambda b, s, pt: (b, 0, 0)
    bias_rows = jnp.tile(bias, nq)[:, None].astype(F32)
    grid_spec = pltpu.PrefetchScalarGridSpec(
        num_scalar_prefetch=1,
        grid=(bsz, steps),
        in_specs=[pl.BlockSpec((rows, 1), lambda b, s, pt: (0, 0)),
                  pl.BlockSpec((1, nq, d), tok),
                  pl.BlockSpec((1, nq, d), tok),
                  pl.BlockSpec((1, nq, d), tok)]
                 + [page_spec(i) for i in range(pps)] + [page_spec(i) for i in range(pps)],
        out_specs=pl.BlockSpec((1, nq, d), tok),
        scratch_shapes=[pltpu.VMEM((rows, d), BF16),
                        pltpu.VMEM((rows, d), F32),
                        pltpu.VMEM((rows, 1), F32),
                        pltpu.VMEM((psize, d), F32),
                        pltpu.VMEM((psize, d), F32)],
    )
    return pl.pallas_call(
        functools.partial(_sb_decode_kernel, heads=heads, pages_per_step=pps, psize=psize, nq=nq),
        grid_spec=grid_spec,
        out_shape=jax.ShapeDtypeStruct((bsz, nq, d), F32),
        compiler_params=_cparams(("parallel", "arbitrary")),
        name="sb_decode",
    )(page_table, bias_rows, q, k_new, v_new, *([cache_k] * pps), *([cache_v] * pps))


def _trunk(x, conv_state, hgrn_state, cache_k, cache_v, page_table, w, tm):
    bsz, seq, d = x.shape
    depth = w['norm_mix'].shape[0]
    heads = w['sb_bias'].shape[1]
    hd = d // heads
    xf = x.reshape(bsz * seq, d)
    tm_wide = 2 * tm if (bsz * seq) % (2 * tm) == 0 else tm
    new_conv, new_hgrn, new_k, new_v = [], [], [], []
    for l in range(depth):
        if l % 2 == 0:
            e = l // 2
            proj = _norm_matmul(xf, w['norm_mix'][l][None, :], w['w_in_even'][e], tm_wide)
            ab, cso, hso = _even_mixer(proj.reshape(bsz, seq, -1), conv_state[e], hgrn_state[e],
                                       w['hgrn_lb_logits'], w['conv_w'][e], w['conv_b'][e],
                                       w['conv_ln_g'][e], w['conv_ln_b'][e], w['hgrn_norm_g'][e], e)
            new_conv.append(cso)
            new_hgrn.append(hso)
            mix_in, w_mix = ab.reshape(bsz * seq, d), w['w_out_even'][e]
        else:
            a = l // 2
            q, k, v, kb, vb = _qkv_proj(xf, w['norm_mix'][l][None, :], w['w_qkv'][a],
                                        w['q_norm_g'][a], w['k_norm_g'][a], heads, tm_wide)
            if cache_k is None:
                o = _sb_prompt(q.reshape(bsz, seq, d), kb.reshape(bsz, seq, d), vb.reshape(bsz, seq, d),
                               w['sb_bias'][a], hd)
            else:
                as_rows = lambda t: t.astype(F32).reshape(bsz, seq, d)
                o = _sb_decode(as_rows(q), as_rows(kb), as_rows(vb),
                               cache_k, cache_v, page_table, w['sb_bias'][a], a, heads)
            new_k.append(k.reshape(bsz, seq, heads, hd))
            new_v.append(v.reshape(bsz, seq, heads, hd))
            mix_in, w_mix = o.reshape(bsz * seq, d), w['w_o'][a]
        xf = _mix_ffn(xf, mix_in, w_mix, w['norm_ffn'][l][None, :], w['w_up'][l], w['w_down'][l], tm)
    return (xf.reshape(bsz, seq, d), jnp.stack(new_conv), jnp.stack(new_hgrn),
            jnp.stack(new_k), jnp.stack(new_v))


def kernel(x_prompt, x_sample, state_conv, state_hgrn, cache_k, cache_v, page_table, norm_mix, norm_ffn, w_in_even, conv_w, conv_b, conv_ln_g, conv_ln_b, hgrn_lb_logits, hgrn_norm_g, w_out_even, w_qkv, q_norm_g, k_norm_g, sb_bias, w_o, w_up, w_down):
    w = dict(norm_mix=norm_mix, norm_ffn=norm_ffn, conv_w=conv_w, conv_b=conv_b, conv_ln_g=conv_ln_g,
             conv_ln_b=conv_ln_b, hgrn_lb_logits=hgrn_lb_logits, hgrn_norm_g=hgrn_norm_g,
             q_norm_g=q_norm_g, k_norm_g=k_norm_g, sb_bias=sb_bias)
    for name, t in dict(w_in_even=w_in_even, w_out_even=w_out_even, w_qkv=w_qkv, w_o=w_o,
                        w_up=w_up, w_down=w_down).items():
        w[name] = t.astype(BF16)
    bsz, seq, d = x_prompt.shape
    n_mix = state_conv.shape[0]
    conv0 = jnp.zeros((n_mix, bsz) + state_conv.shape[2:], F32)
    hgrn0 = jnp.zeros((n_mix, bsz) + state_hgrn.shape[2:], F32)
    tm_p = 256 if (bsz * seq) % 256 == 0 else bsz * seq
    y_p, conv_p, hgrn_p, k_p, v_p = _trunk(x_prompt, conv0, hgrn0, None, None, None, w, tm_p)
    sb, ss, _ = x_sample.shape
    lay, pages, psize = cache_k.shape[:3]
    ck = jnp.transpose(cache_k, (0, 1, 3, 4, 2)).reshape(lay, pages, d, psize)
    cv = jnp.transpose(cache_v, (0, 1, 3, 4, 2)).reshape(lay, pages, d, psize)
    y_s, conv_s, hgrn_s, k_s, v_s = _trunk(x_sample, state_conv, state_hgrn, ck, cv, page_table, w, sb * ss)
    return (y_p, y_s, conv_p, conv_s, hgrn_p, hgrn_s, k_p, v_p, k_s, v_s)
```

```python
import functools

import jax
import jax.numpy as jnp
from jax import lax
from jax.experimental import pallas as pl
from jax.experimental.pallas import tpu as pltpu

F32 = jnp.float32
BF16 = jnp.bfloat16

EPS = 1e-6
HGRN_CHUNK = 128
HGRN_BASE = 16
CONV_HALO = 32
ATT_BLK = 256
ATT_LANES = 256
DECODE_PAGES_PER_STEP = 16
VMEM_LIMIT = 56 * 1024 * 1024
MXU_TILE = 256
SUBLANES = 8
MASKED_LOGW = -1e30

def _cparams(sem):
    return pltpu.CompilerParams(dimension_semantics=sem, vmem_limit_bytes=VMEM_LIMIT)


def _rms(x, g):
    ms = jnp.mean(x * x, axis=-1, keepdims=True)
    return x * lax.rsqrt(ms + EPS) * g


def _dot(a, b):
    return jnp.dot(a, b, preferred_element_type=F32)


def _dot_nt(a, b):
    return lax.dot_general(a, b, (((1,), (1,)), ((), ())), preferred_element_type=F32)


def _split2(x):
    hi = x.astype(BF16)
    lo = (x - hi.astype(F32)).astype(BF16)
    return hi, lo


def _split3(x):
    hi = x.astype(BF16)
    r = x - hi.astype(F32)
    mid = r.astype(BF16)
    lo = (r - mid.astype(F32)).astype(BF16)
    return hi, mid, lo


def _norm_matmul_kernel(x_ref, g_ref, w_ref, o_ref):
    h = _rms(x_ref[...], g_ref[...]).astype(BF16)
    o_ref[...] = _dot(h, w_ref[...])


def _norm_matmul(x, g, w, tm):
    m, d = x.shape
    n = w.shape[1]
    return pl.pallas_call(
        _norm_matmul_kernel,
        grid=(m // tm,),
        in_specs=[pl.BlockSpec((tm, d), lambda i: (i, 0)),
                  pl.BlockSpec((1, d), lambda i: (0, 0)),
                  pl.BlockSpec((d, n), lambda i: (0, 0))],
        out_specs=pl.BlockSpec((tm, n), lambda i: (i, 0)),
        out_shape=jax.ShapeDtypeStruct((m, n), F32),
        compiler_params=_cparams(("parallel",)),
        name="norm_matmul",
    )(x, g, w)


def _qkv_kernel(x_ref, g_ref, w_ref, qg_ref, kg_ref, q_ref, k_ref, v_ref, kb_ref, vb_ref,
                *, d, head_dim, q_scale):
    h = _rms(x_ref[...], g_ref[...]).astype(BF16)
    grp = MXU_TILE
    ri = lax.broadcasted_iota(jnp.int32, (2 * grp, grp), 0) & (grp - 1)
    ci = lax.broadcasted_iota(jnp.int32, (2 * grp, grp), 1)
    same_head2 = jnp.where(ri // head_dim == ci // head_dim, 1.0, 0.0).astype(BF16)

    def head_norm(t, gain):
        t2 = t * t
        parts = []
        for c in range(d // grp):
            hi, lo = _split2(t2[:, c * grp:(c + 1) * grp])
            parts.append(_dot(jnp.concatenate([hi, lo], axis=-1), same_head2))
        ss = jnp.concatenate(parts, axis=-1)
        return t * lax.rsqrt(ss * (1.0 / head_dim) + EPS) * gain

    q = head_norm(_dot(h, w_ref[:, 0:d]), qg_ref[...])
    q_ref[...] = (q * q_scale).astype(BF16)
    k = head_norm(_dot(h, w_ref[:, d:2 * d]), kg_ref[...])
    k_ref[...] = k.reshape(k_ref.shape)
    kb_ref[...] = k.astype(BF16)
    v = _dot(h, w_ref[:, 2 * d:3 * d])
    v_ref[...] = v.reshape(v_ref.shape)
    vb_ref[...] = v.astype(BF16)


def _qkv_proj(x, g, w, qg, kg, heads, tm):
    m, d = x.shape
    head_dim = d // heads
    assert d % MXU_TILE == 0 and MXU_TILE % head_dim == 0
    row = lambda i: (i, 0)
    fixed = lambda i: (0, 0)
    outs = pl.pallas_call(
        functools.partial(_qkv_kernel, d=d, head_dim=head_dim, q_scale=float(head_dim) ** -0.5),
        grid=(m // tm,),
        in_specs=[pl.BlockSpec((tm, d), row),
                  pl.BlockSpec((1, d), fixed),
                  pl.BlockSpec((d, 3 * d), fixed),
                  pl.BlockSpec((1, d), fixed),
                  pl.BlockSpec((1, d), fixed)],
        out_specs=[pl.BlockSpec((tm, d), row),
                   pl.BlockSpec((tm, heads, head_dim), lambda i: (i, 0, 0)),
                   pl.BlockSpec((tm, heads, head_dim), lambda i: (i, 0, 0)),
                   pl.BlockSpec((tm, d), row),
                   pl.BlockSpec((tm, d), row)],
        out_shape=[jax.ShapeDtypeStruct((m, d), BF16),
                   jax.ShapeDtypeStruct((m, heads, head_dim), F32),
                   jax.ShapeDtypeStruct((m, heads, head_dim), F32),
                   jax.ShapeDtypeStruct((m, d), BF16),
                   jax.ShapeDtypeStruct((m, d), BF16)],
        compiler_params=_cparams(("parallel",)),
        name="qkv_proj",
    )(x, g, w, jnp.tile(qg, heads)[None, :], jnp.tile(kg, heads)[None, :])
    return outs


def _mix_ffn_kernel(x_ref, m_ref, wo_ref, g_ref, wu_ref, wd_ref, o_ref, *, ff_chunk):
    x1 = x_ref[...] + _dot(m_ref[...].astype(BF16), wo_ref[...])
    h = _rms(x1, g_ref[...]).astype(BF16)
    acc = x1
    for c in range(wu_ref.shape[1] // ff_chunk):
        u = _dot(h, wu_ref[:, c * ff_chunk:(c + 1) * ff_chunk])
        u = jnp.square(jnp.maximum(u, 0.0)).astype(BF16)
        acc = acc + _dot(u, wd_ref[c * ff_chunk:(c + 1) * ff_chunk, :])
    o_ref[...] = acc


def _mix_ffn(x, mix_in, wo, g, wu, wd, tm):
    m, d = x.shape
    dff = wu.shape[1]
    row = lambda i: (i, 0)
    fixed = lambda i: (0, 0)
    once = pl.Buffered(1)
    return pl.pallas_call(
        functools.partial(_mix_ffn_kernel, ff_chunk=min(dff, 1024)),
        grid=(m // tm,),
        in_specs=[pl.BlockSpec((tm, d), row),
                  pl.BlockSpec((tm, d), row),
                  pl.BlockSpec((d, d), fixed, pipeline_mode=once),
                  pl.BlockSpec((1, d), fixed),
                  pl.BlockSpec((d, dff), fixed, pipeline_mode=once),
                  pl.BlockSpec((dff, d), fixed, pipeline_mode=once)],
        out_specs=pl.BlockSpec((tm, d), row),
        out_shape=jax.ShapeDtypeStruct((m, d), F32),
        compiler_params=_cparams(("parallel",)),
        name="mix_ffn",
    )(x, mix_in, wo, g, wu, wd)


def _sigmoid_pair(z):
    e = jnp.exp(-jnp.abs(z))
    inv = 1.0 / (1.0 + e)
    pos = z >= 0
    return jnp.where(pos, 1.0, e) * inv, jnp.where(pos, e, 1.0) * inv


def _even_mixer_kernel(p_ref, cs_ref, hs_ref, lbl_ref, cw_ref, cb_ref, lng_ref, lnb_ref, ng_ref,
                       ab_ref, cso_ref, hso_ref, ubuf, st, pbuf, aligned,
                       *, layer, chunk, rows, dc, heads, dk, taps):
    C = chunk
    hk = heads * dk
    t_idx = pl.program_id(1)
    hist = taps - 1

    @pl.when(t_idx == 0)
    def _init():
        ubuf[...] = jnp.zeros_like(ubuf)
        ubuf[pl.ds(CONV_HALO - hist, hist), :] = cs_ref[0]
        for h in range(heads):
            st[h] = hs_ref[0, h].T

    if rows < C:
        pbuf[...] = jnp.zeros_like(pbuf)
        pbuf[pl.ds(0, rows), :] = p_ref[0]
        src = pbuf
    else:
        src = p_ref.at[0]

    val = src[:, 0:dc]
    gate = src[:, dc:2 * dc]
    u = val * _sigmoid_pair(gate)[0]
    ubuf[pl.ds(CONV_HALO, C), :] = u
    cw = cw_ref[...]
    y = jnp.zeros((C, dc), F32) + cb_ref[...]
    first = CONV_HALO - hist
    for phase in range(SUBLANES):
        offs = [first + j for j in range(taps) if (first + j) % SUBLANES == phase]
        if not offs:
            continue
        n = offs[-1] - offs[0] + C
        aligned[pl.ds(0, n), :] = ubuf[pl.ds(offs[0], n), :]
        for o in offs:
            y = y + cw[o - first:o - first + 1, :] * aligned[pl.ds(o - offs[0], C), :]
    mu = jnp.mean(y, axis=-1, keepdims=True)
    yc = y - mu
    var = jnp.mean(yc * yc, axis=-1, keepdims=True)
    y = yc * lax.rsqrt(var + EPS) * lng_ref[...] + lnb_ref[...]
    a_out = y * _sigmoid_pair(y)[0]

    @pl.when(t_idx == pl.num_programs(1) - 1)
    def _conv_state_out():
        cso_ref[0] = ubuf[pl.ds(CONV_HALO + rows - hist, hist), :]

    ubuf[pl.ds(0, CONV_HALO), :] = ubuf[pl.ds(C, CONV_HALO), :]

    lbl = lbl_ref[...]
    lmax = jnp.max(lbl, axis=0, keepdims=True)
    pe = jnp.exp(lbl - lmax)
    psm = pe / jnp.sum(pe, axis=0, keepdims=True)
    lb = jnp.zeros((1, hk), F32)
    for i in range(1, layer + 1):
        lb = lb + psm[i:i + 1, :]

    o0 = 2 * dc
    q = src[:, o0:o0 + hk]
    z = src[:, o0 + hk:o0 + 2 * hk]
    vin = src[:, o0 + 2 * hk:o0 + 3 * hk]
    og = src[:, o0 + 3 * hk:o0 + 4 * hk]
    sig, nsig = _sigmoid_pair(z)
    f = lb + (1.0 - lb) * sig
    k = (1.0 - lb) * nsig
    logf = jnp.log(f)
    row = lax.broadcasted_iota(jnp.int32, (C, hk), 0)
    if rows < C:
        valid = row < rows
        logf = jnp.where(valid, logf, 0.0)
        k = jnp.where(valid, k, 0.0)
    ri = lax.broadcasted_iota(jnp.int32, (C, C), 0)
    ci = lax.broadcasted_iota(jnp.int32, (C, C), 1)
    tril = jnp.where(ri >= ci, 1.0, 0.0).astype(BF16)
    hi, mid, lo = _split3(logf)
    b = _dot(tril, hi) + _dot(tril, mid) + _dot(tril, lo)

    def row_bcast(r, n):
        return jnp.broadcast_to(b[r:r + 1, :], (n, hk))

    b_end = b[C - 1:C, :]
    q_inter = (q * jnp.exp(b)).astype(BF16)
    k_end = (k * jnp.exp(b_end - b)).astype(BF16)
    dec = jnp.exp(b_end)
    levels = []
    m = C // 2
    while m >= HGRN_BASE:
        ref = jnp.concatenate([row_bcast(blk * 2 * m + m - 1, 2 * m) for blk in range(C // (2 * m))], axis=0)
        is_q = (row & (2 * m - 1)) >= m
        e = jnp.where(is_q, b - ref, ref - b)
        levels.append((m, (jnp.where(is_q, q, k) * jnp.exp(e)).astype(BF16)))
        m //= 2
    base = jnp.concatenate(
        [jnp.zeros((HGRN_BASE, hk), F32)]
        + [row_bcast(blk * HGRN_BASE - 1, HGRN_BASE) for blk in range(1, C // HGRN_BASE)], axis=0)
    bl = b - base
    q_diag = (q * jnp.exp(bl)).astype(BF16)
    k_diag = (k * jnp.exp(-bl)).astype(BF16)
    xor = ri ^ ci
    causal = ci <= ri
    vb = vin.astype(BF16)

    outs = []
    for h in range(heads):
        sl = slice(h * dk, (h + 1) * dk)
        att = _dot_nt(q_diag[:, sl], k_diag[:, sl])
        for m, x in reversed(levels):
            att = jnp.where(xor < m, att, _dot_nt(x[:, sl], x[:, sl]))
        att = jnp.where(causal, att, 0.0)
        s_t = st[h]
        o = _dot(att.astype(BF16), vb[:, sl]) + _dot_nt(q_inter[:, sl], s_t.astype(BF16))
        st[h] = s_t * dec[:, sl] + _dot(vin[:, sl].T.astype(BF16), k_end[:, sl])
        o = _rms(o, ng_ref[...])
        g = og[:, sl]
        outs.append(o * (g * _sigmoid_pair(g)[0]))
    b_out = jnp.concatenate(outs, axis=-1)
    ab = jnp.concatenate([a_out, b_out], axis=-1)
    ab_ref[0] = ab[0:rows, :].astype(ab_ref.dtype)

    @pl.when(t_idx == pl.num_programs(1) - 1)
    def _hgrn_state_out():
        for h in range(heads):
            hso_ref[0, h] = st[h].T


def _even_mixer(proj, conv_state, hgrn_state, lb_logits, cw, cb, lng, lnb, ng, layer):
    bsz, seq, width = proj.shape
    taps, dc = cw.shape
    heads, dk, dv = hgrn_state.shape[1:]
    assert dk == dv and width == 2 * dc + 4 * heads * dk
    chunk = HGRN_CHUNK if seq >= HGRN_CHUNK else HGRN_BASE
    rows = min(seq, chunk)
    assert seq % rows == 0
    nt = seq // rows
    kern = functools.partial(_even_mixer_kernel, layer=layer, chunk=chunk, rows=rows, dc=dc, heads=heads, dk=dk,
                             taps=taps)
    fixed2 = lambda b, t: (0, 0)
    return pl.pallas_call(
        kern,
        grid=(bsz, nt),
        in_specs=[pl.BlockSpec((1, rows, width), lambda b, t: (b, t, 0)),
                  pl.BlockSpec((1, taps - 1, dc), lambda b, t: (b, 0, 0)),
                  pl.BlockSpec((1, heads, dk, dv), lambda b, t: (b, 0, 0, 0)),
                  pl.BlockSpec(lb_logits.shape, fixed2),
                  pl.BlockSpec((taps, dc), fixed2),
                  pl.BlockSpec((1, dc), fixed2),
                  pl.BlockSpec((1, dc), fixed2),
                  pl.BlockSpec((1, dc), fixed2),
                  pl.BlockSpec((1, dv), fixed2)],
        out_specs=[pl.BlockSpec((1, rows, dc + heads * dv), lambda b, t: (b, t, 0)),
                   pl.BlockSpec((1, taps - 1, dc), lambda b, t: (b, 0, 0)),
                   pl.BlockSpec((1, heads, dk, dv), lambda b, t: (b, 0, 0, 0))],
        out_shape=[jax.ShapeDtypeStruct((bsz, seq, dc + heads * dv), BF16 if rows % 16 == 0 else F32),
                   jax.ShapeDtypeStruct((bsz, taps - 1, dc), F32),
                   jax.ShapeDtypeStruct((bsz, heads, dk, dv), F32)],
        scratch_shapes=[pltpu.VMEM((CONV_HALO + chunk + CONV_HALO, dc), F32),
                        pltpu.VMEM((heads, dv, dk), F32),
                        pltpu.VMEM((chunk, width), F32),
                        pltpu.VMEM((chunk + CONV_HALO, dc), F32)],
        compiler_params=_cparams(("parallel", "arbitrary")),
        name="even_mixer",
    )(proj, conv_state, hgrn_state, lb_logits, cw, cb[None, :], lng[None, :], lnb[None, :], ng[None, :])


def _softplus(z):
    return jnp.maximum(z, 0.0) + jnp.log(1.0 + jnp.exp(-jnp.abs(z)))


def _sb_weights(z, vis, carry, suffix2):
    sp = _softplus(z)
    if vis is not None:
        sp = jnp.where(vis, sp, 0.0)
    incl = _dot(jnp.concatenate(_split2(sp), axis=-1), suffix2)
    w = jnp.exp(z - carry - incl)
    if vis is not None:
        w = jnp.where(vis, w, 0.0)
    return w, carry + jnp.sum(sp, axis=-1, keepdims=True)


def _suffix2(n):
    j = lax.broadcasted_iota(jnp.int32, (2 * n, n), 0) & (n - 1)
    s = lax.broadcasted_iota(jnp.int32, (2 * n, n), 1)
    return jnp.where(j >= s, 1.0, 0.0).astype(BF16)


def _sb_prompt_kernel(bias_ref, q_ref, k_ref, v_ref, o_ref, vm, *, blk, hd):
    lanes = q_ref.shape[-1]
    nh = lanes // hd
    hp = pl.program_id(1)
    qi = pl.program_id(2)
    lane_head = lax.broadcasted_iota(jnp.int32, (blk, lanes), 1) // hd

    @pl.when(qi == 0)
    def _mask_values():
        def fill(r, _):
            start = pl.multiple_of(r * blk, blk)
            vf = v_ref[0, pl.ds(start, blk), :].astype(F32)
            for j in range(nh):
                vm[j, pl.ds(start, blk), :] = jnp.where(lane_head == j, vf, 0.0).astype(BF16)
            return 0
        lax.fori_loop(0, v_ref.shape[1] // blk, fill, 0)

    qf = q_ref[0].astype(F32)
    qs = [jnp.where(lane_head == j, qf, 0.0).astype(BF16) for j in range(nh)]
    biases = [bias_ref[hp * nh + j] for j in range(nh)]
    suffix2 = _suffix2(blk)
    row = lax.broadcasted_iota(jnp.int32, (blk, blk), 0)
    col = lax.broadcasted_iota(jnp.int32, (blk, blk), 1)

    def scores(kbs):
        starts = [pl.multiple_of(kb * blk, blk) for kb in kbs]
        return starts, [[_dot_nt(qs[j], k_ref[0, pl.ds(st, blk), :]) + biases[j] for j in range(nh)]
                        for st in starts]

    def soften(zs, cs, vis):
        sps = [[_softplus(z) for z in zk] for zk in zs]
        if vis is not None:
            sps[0] = [jnp.where(vis, sp, 0.0) for sp in sps[0]]
        sums = [[jnp.sum(sp, axis=-1, keepdims=True) for sp in sk] for sk in sps]
        zcs = []
        for k in range(len(zs)):
            zcs.append([zs[k][j] - cs[j] for j in range(nh)])
            cs = [cs[j] + sums[k][j] for j in range(nh)]
        if vis is not None:
            zcs[0] = [jnp.where(vis, zc, MASKED_LOGW) for zc in zcs[0]]
        return sps, zcs, cs

    def suffix_sums(sps):
        return [[_dot(jnp.concatenate(_split2(sp), axis=-1), suffix2) for sp in sk] for sk in sps]

    def weigh(starts, zcs, incl, acc):
        ws = [[jnp.exp(zcs[k][j] - incl[k][j]).astype(BF16) for j in range(nh)] for k in range(len(starts))]
        w_all = jnp.concatenate([ws[k][j] for k in range(len(starts)) for j in range(nh)], axis=1)
        v_all = jnp.concatenate([vm[j, pl.ds(st, blk), :] for st in starts for j in range(nh)], axis=0)
        return acc + _dot(w_all, v_all)

    def blocks(kbs, carry, vis):
        cs, acc = carry
        starts, zs = scores(kbs)
        sps, zcs, cs = soften(zs, cs, vis)
        return cs, weigh(starts, zcs, suffix_sums(sps), acc)

    def two_groups(kbs1, kbs2, carry):
        cs, acc = carry
        starts1, zs1 = scores(kbs1)
        sps1, zcs1, cs = soften(zs1, cs, None)
        incl1 = suffix_sums(sps1)
        starts2, zs2 = scores(kbs2)
        acc = weigh(starts1, zcs1, incl1, acc)
        sps2, zcs2, cs = soften(zs2, cs, None)
        return cs, weigh(starts2, zcs2, suffix_sums(sps2), acc)

    carry = ([jnp.zeros((blk, 1), F32) for _ in range(nh)], jnp.zeros((blk, lanes), F32))
    carry = blocks([qi], carry, col < row)
    odd = qi % 2
    carry = lax.fori_loop(0, odd, lambda i, cr: blocks([qi - 1], cr, None), carry)
    top = qi - odd
    odd_pair = (top // 2) % 2
    carry = lax.fori_loop(0, odd_pair, lambda i, cr: blocks([top - 1, top - 2], cr, None), carry)
    top = top - 2 * odd_pair
    _, acc = lax.fori_loop(
        0, top // 4,
        lambda i, cr: two_groups([top - 1 - 4 * i, top - 2 - 4 * i], [top - 3 - 4 * i, top - 4 - 4 * i], cr), carry)
    o_ref[0] = acc.astype(o_ref.dtype)


def _sb_prompt(q, k, v, bias, hd):
    bsz, seq, d = q.shape
    lanes = ATT_LANES
    blk = min(ATT_BLK, seq)
    assert seq % blk == 0 and d % lanes == 0 and lanes % hd == 0 and blk & (blk - 1) == 0
    return pl.pallas_call(
        functools.partial(_sb_prompt_kernel, blk=blk, hd=hd),
        grid=(bsz, d // lanes, seq // blk),
        in_specs=[pl.BlockSpec(memory_space=pltpu.SMEM),
                  pl.BlockSpec((1, blk, lanes), lambda b, h, i: (b, i, h)),
                  pl.BlockSpec((1, seq, lanes), lambda b, h, i: (b, 0, h)),
                  pl.BlockSpec((1, seq, lanes), lambda b, h, i: (b, 0, h))],
        out_specs=pl.BlockSpec((1, blk, lanes), lambda b, h, i: (b, i, h)),
        out_shape=jax.ShapeDtypeStruct((bsz, seq, d), BF16),
        scratch_shapes=[pltpu.VMEM((lanes // hd, seq, lanes), BF16)],
        compiler_params=_cparams(("parallel", "parallel", "arbitrary")),
        name="sb_prompt",
    )(bias, q, k, v)


def _sb_decode_kernel(pt_ref, bias_ref, q_ref, kn_ref, vn_ref, *rest, heads, pages_per_step, psize, nq):
    kp = rest[:pages_per_step]
    vp = rest[pages_per_step:2 * pages_per_step]
    o_ref = rest[2 * pages_per_step]
    qbd, acc, cbuf, kpad, vpad = rest[2 * pages_per_step + 1:]
    del pt_ref
    step = pl.program_id(1)
    d = q_ref.shape[-1]
    hd = d // heads
    rows = nq * heads
    lane_head = lax.broadcasted_iota(jnp.int32, (rows, d), 1) // hd
    row_head = lax.broadcasted_iota(jnp.int32, (rows, d), 0) & (heads - 1)
    own = lane_head == row_head
    suffix2 = _suffix2(psize)
    bias = bias_ref[...]

    def attend_new_rows(kblk, vblk, vis):
        z = _dot_nt(qbd[...], kblk) + bias
        w, c = _sb_weights(z, vis, cbuf[...], suffix2)
        acc[...] += _dot(w.astype(BF16), vblk)
        cbuf[...] = c

    @pl.when(step == 0)
    def _new_rows():
        qv = q_ref[0]
        qbd[...] = jnp.where(own, jnp.concatenate(
            [jnp.broadcast_to(qv[t:t + 1, :], (heads, d)) for t in range(nq)], axis=0), 0.0).astype(BF16)
        acc[...] = jnp.zeros_like(acc)
        cbuf[...] = jnp.zeros_like(cbuf)
        kpad[...] = jnp.zeros_like(kpad)
        vpad[...] = jnp.zeros_like(vpad)
        kpad[pl.ds(0, nq), :] = kn_ref[0]
        vpad[pl.ds(0, nq), :] = vn_ref[0]
        s_idx = lax.broadcasted_iota(jnp.int32, (rows, psize), 1)
        t_idx = lax.broadcasted_iota(jnp.int32, (rows, psize), 0) // heads
        attend_new_rows(kpad[...].astype(BF16), vpad[...].astype(BF16), s_idx < t_idx)

    q_bd = qbd[...]
    zs = [_dot(q_bd, kp[i][...].astype(BF16)) + bias for i in range(pages_per_step)]
    sps = [_softplus(z) for z in zs]
    c = cbuf[...]
    zcs = []
    for i in range(pages_per_step):
        zcs.append(zs[i] - c)
        c = c + jnp.sum(sps[i], axis=-1, keepdims=True)
    cbuf[...] = c
    incl = [_dot(jnp.concatenate(_split2(sp), axis=-1), suffix2) for sp in sps]
    ws = [jnp.exp(zcs[i] - incl[i]).astype(BF16) for i in range(pages_per_step)]
    out = _dot_nt(ws[0], vp[0][...].astype(BF16))
    for i in range(1, pages_per_step):
        out = out + _dot_nt(ws[i], vp[i][...].astype(BF16))
    acc[...] += out

    @pl.when(step == pl.num_programs(1) - 1)
    def _finish():
        a = jnp.where(own, acc[...], 0.0)
        o_ref[0] = jnp.concatenate(
            [jnp.sum(a[t * heads:(t + 1) * heads, :], axis=0, keepdims=True) for t in range(nq)],
            axis=0).astype(o_ref.dtype)


def _sb_decode(q, k_new, v_new, cache_k, cache_v, page_table, bias, layer, heads):
    bsz, nq, d = q.shape
    n_pages = page_table.shape[1]
    psize = cache_k.shape[3]
    pps = DECODE_PAGES_PER_STEP if n_pages % DECODE_PAGES_PER_STEP == 0 else 1
    steps = n_pages // pps
    rows = nq * heads
    assert heads & (heads - 1) == 0 and psize & (psize - 1) == 0 and cache_k.shape[2] == d

    def page_spec(i):
        def imap(b, s, pt):
            return (layer, pt[b, n_pages - 1 - (s * pps + i)], 0, 0)
        return pl.BlockSpec((None, None, d, psize), imap)

    tok = lambda b, s, pt: (b, 0, 0)
    bias_rows = jnp.tile(bias, nq)[:, None].astype(F32)
    grid_spec = pltpu.PrefetchScalarGridSpec(
        num_scalar_prefetch=1,
        grid=(bsz, steps),
        in_specs=[pl.BlockSpec((rows, 1), lambda b, s, pt: (0, 0)),
                  pl.BlockSpec((1, nq, d), tok),
                  pl.BlockSpec((1, nq, d), tok),
                  pl.BlockSpec((1, nq, d), tok)]
                 + [page_spec(i) for i in range(pps)] + [page_spec(i) for i in range(pps)],
        out_specs=pl.BlockSpec((1, nq, d), tok),
        scratch_shapes=[pltpu.VMEM((rows, d), BF16),
                        pltpu.VMEM((rows, d), F32),
                        pltpu.VMEM((rows, 1), F32),
                        pltpu.VMEM((psize, d), F32),
                        pltpu.VMEM((psize, d), F32)],
    )
    return pl.pallas_call(
        functools.partial(_sb_decode_kernel, heads=heads, pages_per_step=pps, psize=psize, nq=nq),
        grid_spec=grid_spec,
        out_shape=jax.ShapeDtypeStruct((bsz, nq, d), F32),
        compiler_params=_cparams(("parallel", "arbitrary")),
        name="sb_decode",
    )(page_table, bias_rows, q, k_new, v_new, *([cache_k] * pps), *([cache_v] * pps))


def _trunk(x, conv_state, hgrn_state, cache_k, cache_v, page_table, w, tm):
    bsz, seq, d = x.shape
    depth = w['norm_mix'].shape[0]
    heads = w['sb_bias'].shape[1]
    hd = d // heads
    xf = x.reshape(bsz * seq, d)
    tm_wide = 2 * tm if (bsz * seq) % (2 * tm) == 0 else tm
    new_conv, new_hgrn, new_k, new_v = [], [], [], []
    for l in range(depth):
        if l % 2 == 0:
            e = l // 2
            proj = _norm_matmul(xf, w['norm_mix'][l][None, :], w['w_in_even'][e], tm_wide)
            ab, cso, hso = _even_mixer(proj.reshape(bsz, seq, -1), conv_state[e], hgrn_state[e],
                                       w['hgrn_lb_logits'], w['conv_w'][e], w['conv_b'][e],
                                       w['conv_ln_g'][e], w['conv_ln_b'][e], w['hgrn_norm_g'][e], e)
            new_conv.append(cso)
            new_hgrn.append(hso)
            mix_in, w_mix = ab.reshape(bsz * seq, d), w['w_out_even'][e]
        else:
            a = l // 2
            q, k, v, kb, vb = _qkv_proj(xf, w['norm_mix'][l][None, :], w['w_qkv'][a],
                                        w['q_norm_g'][a], w['k_norm_g'][a], heads, tm_wide)
            if cache_k is None:
                o = _sb_prompt(q.reshape(bsz, seq, d), kb.reshape(bsz, seq, d), vb.reshape(bsz, seq, d),
                               w['sb_bias'][a], hd)
            else:
                as_rows = lambda t: t.astype(F32).reshape(bsz, seq, d)
                o = _sb_decode(as_rows(q), as_rows(kb), as_rows(vb),
                               cache_k, cache_v, page_table, w['sb_bias'][a], a, heads)
            new_k.append(k.reshape(bsz, seq, heads, hd))
            new_v.append(v.reshape(bsz, seq, heads, hd))
            mix_in, w_mix = o.reshape(bsz * seq, d), w['w_o'][a]
        xf = _mix_ffn(xf, mix_in, w_mix, w['norm_ffn'][l][None, :], w['w_up'][l], w['w_down'][l], tm_wide)
    return (xf.reshape(bsz, seq, d), jnp.stack(new_conv), jnp.stack(new_hgrn),
            jnp.stack(new_k), jnp.stack(new_v))


def kernel(x_prompt, x_sample, state_conv, state_hgrn, cache_k, cache_v, page_table, norm_mix, norm_ffn, w_in_even, conv_w, conv_b, conv_ln_g, conv_ln_b, hgrn_lb_logits, hgrn_norm_g, w_out_even, w_qkv, q_norm_g, k_norm_g, sb_bias, w_o, w_up, w_down):
    w = dict(norm_mix=norm_mix, norm_ffn=norm_ffn, conv_w=conv_w, conv_b=conv_b, conv_ln_g=conv_ln_g,
             conv_ln_b=conv_ln_b, hgrn_lb_logits=hgrn_lb_logits, hgrn_norm_g=hgrn_norm_g,
             q_norm_g=q_norm_g, k_norm_g=k_norm_g, sb_bias=sb_bias)
    for name, t in dict(w_in_even=w_in_even, w_out_even=w_out_even, w_qkv=w_qkv, w_o=w_o,
                        w_up=w_up, w_down=w_down).items():
        w[name] = t.astype(BF16)
    bsz, seq, d = x_prompt.shape
    n_mix = state_conv.shape[0]
    conv0 = jnp.zeros((n_mix, bsz) + state_conv.shape[2:], F32)
    hgrn0 = jnp.zeros((n_mix, bsz) + state_hgrn.shape[2:], F32)
    tm_p = 256 if (bsz * seq) % 256 == 0 else bsz * seq
    y_p, conv_p, hgrn_p, k_p, v_p = _trunk(x_prompt, conv0, hgrn0, None, None, None, w, tm_p)
    sb, ss, _ = x_sample.shape
    lay, pages, psize = cache_k.shape[:3]
    ck = jnp.transpose(cache_k, (0, 1, 3, 4, 2)).reshape(lay, pages, d, psize)
    cv = jnp.transpose(cache_v, (0, 1, 3, 4, 2)).reshape(lay, pages, d, psize)
    y_s, conv_s, hgrn_s, k_s, v_s = _trunk(x_sample, state_conv, state_hgrn, ck, cv, page_table, w, sb * ss)
    return (y_p, y_s, conv_p, conv_s, hgrn_p, hgrn_s, k_p, v_p, k_s, v_s)
```
